```python
import jax, jax.numpy as jnp
from jax import lax
import numpy as np

D_MODEL = 2048
BATCH = 16
SEQ = 2048
DEPTH = 4

D_MIX = D_MODEL
A_WIDTH = D_MIX // 2
B_WIDTH = D_MIX // 4
C_WIDTH = D_MIX - A_WIDTH - B_WIDTH

V_DIM = 128
NOPE_DIM = 128
ROPE_DIM = 64
A_HEADS = A_WIDTH // V_DIM
Q_LORA = D_MODEL // 4
KV_LORA = D_MODEL // 8
ROPE_THETA = 10000.0
Q_BLOCK = 128

B_HEAD = 64
B_HEADS = B_WIDTH // B_HEAD
DECAY_LORA = 32
ICLR_LORA = 32
GATE_LORA = 96
HEAD_NORM_EPS = 64e-5

C_GROUPS = 8
CONV_K = 3

D_FF = ((8 * D_MODEL // 3 + 255) // 256) * 256
RMS_EPS = 1e-6

A_IN = Q_LORA + KV_LORA + ROPE_DIM
B_IN = 3 * B_WIDTH + DECAY_LORA + ICLR_LORA + GATE_LORA
C_IN = 3 * C_WIDTH
N_IN = A_IN + B_IN + C_IN

kernel_name = "hymba_style_mla_rwkv7_shortconv_macaron"


def rmsnorm(x, g, eps=RMS_EPS):
    xf = x.astype(jnp.float32)
    y = xf * lax.rsqrt(jnp.mean(xf * xf, axis=-1, keepdims=True) + eps)
    return (y * g.astype(jnp.float32)).astype(x.dtype)


def group_rmsnorm(y, g, n_groups, eps=RMS_EPS):
    shp = y.shape
    yg = y.reshape(shp[:-1] + (n_groups, shp[-1] // n_groups)).astype(jnp.float32)
    yg = yg * lax.rsqrt(jnp.mean(yg * yg, axis=-1, keepdims=True) + eps)
    return (yg.reshape(shp) * g.astype(jnp.float32)).astype(y.dtype)


def swiglu(x, w_gate, w_up, w_down):
    return (jax.nn.silu(x @ w_gate) * (x @ w_up)) @ w_down


def rope_angles(positions):
    inv_freq = 1.0 / (ROPE_THETA ** (jnp.arange(0, ROPE_DIM, 2, dtype=jnp.float32) / ROPE_DIM))
    ang = positions.astype(jnp.float32)[..., None] * inv_freq
    return jnp.cos(ang), jnp.sin(ang)


def apply_rope(x, cos, sin):
    half = ROPE_DIM // 2
    xf = x.astype(jnp.float32)
    x1, x2 = xf[..., :half], xf[..., half:]
    return jnp.concatenate([x1 * cos - x2 * sin, x1 * sin + x2 * cos], axis=-1).astype(x.dtype)


def causal_block_attention(q_nope, q_rope, k_nope, k_rope, v):
    B, S, H, _ = q_nope.shape
    scale = (NOPE_DIM + ROPE_DIM) ** -0.5
    kpos = jnp.arange(S)

    def one_block(i):
        start = i * Q_BLOCK
        qn = lax.dynamic_slice_in_dim(q_nope, start, Q_BLOCK, axis=1)
        qr = lax.dynamic_slice_in_dim(q_rope, start, Q_BLOCK, axis=1)
        s = (jnp.einsum('bqhd,bkhd->bhqk', qn, k_nope)
             + jnp.einsum('bqhr,bkr->bhqk', qr, k_rope)).astype(jnp.float32) * scale
        qpos = start + jnp.arange(Q_BLOCK)
        mask = kpos[None, :] <= qpos[:, None]
        s = jnp.where(mask[None, None], s, -1e30)
        p = jax.nn.softmax(s, axis=-1).astype(v.dtype)
        return jnp.einsum('bhqk,bkhd->bqhd', p, v)

    out = lax.map(one_block, jnp.arange(S // Q_BLOCK))
    return jnp.moveaxis(out, 0, 1).reshape(B, S, H, v.shape[-1])


def mla_mixer(pa, cos, sin, q_norm, kv_norm, w_uq, w_ukv, out_norm):
    B, S, _ = pa.shape
    c_q = pa[..., :Q_LORA]
    c_kv = pa[..., Q_LORA:Q_LORA + KV_LORA]
    k_rope = pa[..., Q_LORA + KV_LORA:]
    q = (rmsnorm(c_q, q_norm) @ w_uq).reshape(B, S, A_HEADS, NOPE_DIM + ROPE_DIM)
    kv = (rmsnorm(c_kv, kv_norm) @ w_ukv).reshape(B, S, A_HEADS, NOPE_DIM + V_DIM)
    q_nope = q[..., :NOPE_DIM]
    q_rope = apply_rope(q[..., NOPE_DIM:], cos[:, :, None, :], sin[:, :, None, :])
    k_nope, v = kv[..., :NOPE_DIM], kv[..., NOPE_DIM:]
    k_rope = apply_rope(k_rope, cos, sin)
    o = causal_block_attention(q_nope, q_rope, k_nope, k_rope, v)
    return group_rmsnorm(o.reshape(B, S, A_WIDTH), out_norm, A_HEADS)


def wkv7_scan(r, w, k, v, a, b):
    B, S, H, N = r.shape

    def step(state, inp):
        r_t, w_t, k_t, v_t, a_t, b_t = inp
        sa = jnp.einsum('bhvk,bhk->bhv', state, a_t)
        state = (state * w_t[:, :, None, :]
                 + sa[..., None] * b_t[:, :, None, :]
                 + v_t[..., None] * k_t[:, :, None, :])
        return state, jnp.einsum('bhvk,bhk->bhv', state, r_t)

    xs = tuple(jnp.moveaxis(t, 1, 0) for t in (r, w, k, v, a, b))
    _, y = lax.scan(step, jnp.zeros((B, H, N, N), jnp.float32), xs)
    return jnp.moveaxis(y, 0, 1)


def rwkv7_mixer(pb, shift_mu, decay_w0, decay_up, iclr_a0, iclr_up, gate_up,
                k_k, k_a, r_k, lnx_gain, lnx_bias):
    B, S, _ = pb.shape
    prev = jnp.pad(pb, ((0, 0), (1, 0), (0, 0)))[:, :-1]
    pb = pb + (prev - pb) * shift_mu
    o1, o2, o3 = B_WIDTH, 2 * B_WIDTH, 3 * B_WIDTH
    o4, o5 = o3 + DECAY_LORA, o3 + DECAY_LORA + ICLR_LORA
    r, k, v = pb[..., :o1], pb[..., o1:o2], pb[..., o2:o3]
    w_lo, a_lo, g_lo = pb[..., o3:o4], pb[..., o4:o5], pb[..., o5:]

    w_log = -jax.nn.softplus(-(decay_w0 + jnp.tanh(w_lo) @ decay_up)) - 0.5
    a = jax.nn.sigmoid(iclr_a0 + a_lo @ iclr_up)
    g = jax.nn.sigmoid(g_lo) @ gate_up

    heads = lambda t: t.reshape(B, S, B_HEADS, B_HEAD).astype(jnp.float32)
    kk = heads(k * k_k)
    kk = kk / jnp.maximum(jnp.sqrt(jnp.sum(kk * kk, axis=-1, keepdims=True)), 1e-12)
    k = k * (1.0 + (a - 1.0) * k_a)
    rh, kh, vh, ah = heads(r), heads(k), heads(v), heads(a)
    decay = jnp.exp(-jnp.exp(heads(w_log)))

    y = wkv7_scan(rh, decay, kh, vh, -kk, kk * ah)
    mu = jnp.mean(y, axis=-1, keepdims=True)
    var = jnp.mean(jnp.square(y - mu), axis=-1, keepdims=True)
    yn = ((y - mu) * lax.rsqrt(var + HEAD_NORM_EPS)).reshape(B, S, B_WIDTH)
    yn = yn * lnx_gain.astype(jnp.float32) + lnx_bias.astype(jnp.float32)
    bonus = jnp.sum(rh * kh * r_k.astype(jnp.float32), axis=-1, keepdims=True) * vh
    out = (yn + bonus.reshape(B, S, B_WIDTH)) * g.astype(jnp.float32)
    return out.astype(pb.dtype)


def short_conv_mixer(pc, conv_w, out_norm):
    b_gate = pc[..., :C_WIDTH]
    c_gate = pc[..., C_WIDTH:2 * C_WIDTH]
    h = pc[..., 2 * C_WIDTH:]
    u = c_gate * h
    y = lax.conv_general_dilated(u, conv_w[:, None, :], window_strides=(1,),
                                 padding=[(CONV_K - 1, 0)],
                                 dimension_numbers=('NWC', 'WIO', 'NWC'),
                                 feature_group_count=C_WIDTH)
    return group_rmsnorm(b_gate * y, out_norm, C_GROUPS)


def setup_inputs(seed: int = 0) -> dict:
    key = jax.random.key(seed)
    ks = iter(jax.random.split(key, 40))
    f32 = jnp.float32

    def nrm(shape, scale):
        return scale * jax.random.normal(next(ks), shape, f32)

    def gain(shape):
        return 1.0 + nrm(shape, 0.02)

    L = DEPTH
    x = nrm((BATCH, SEQ, D_MODEL), 1.0)
    positions = (jax.random.randint(next(ks), (BATCH, 1), 0, 1024, jnp.int32)
                 + jnp.arange(SEQ, dtype=jnp.int32)[None, :])
    return {
        "x": x,
        "positions": positions,
        "norm_ffn1": gain((L, D_MODEL)),
        "ffn1_gate": nrm((L, D_MODEL, D_FF), D_MODEL ** -0.5),
        "ffn1_up": nrm((L, D_MODEL, D_FF), D_MODEL ** -0.5),
        "ffn1_down": nrm((L, D_FF, D_MODEL), D_FF ** -0.5),
        "norm_mix": gain((L, D_MODEL)),
        "w_in": nrm((L, D_MODEL, N_IN), D_MODEL ** -0.5),
        "q_norm": gain((L, Q_LORA)),
        "kv_norm": gain((L, KV_LORA)),
        "w_uq": nrm((L, Q_LORA, A_HEADS * (NOPE_DIM + ROPE_DIM)), Q_LORA ** -0.5),
        "w_ukv": nrm((L, KV_LORA, A_HEADS * (NOPE_DIM + V_DIM)), KV_LORA ** -0.5),
        "attn_out_norm": gain((L, A_WIDTH)),
        "shift_mu": jax.random.uniform(next(ks), (L, B_IN), f32),
        "decay_w0": -2.0 + nrm((L, B_WIDTH), 0.5),
        "decay_up": nrm((L, DECAY_LORA, B_WIDTH), 0.5 * DECAY_LORA ** -0.5),
        "iclr_a0": nrm((L, B_WIDTH), 0.1),
        "iclr_up": nrm((L, ICLR_LORA, B_WIDTH), ICLR_LORA ** -0.5),
        "gate_up": nrm((L, GATE_LORA, B_WIDTH), GATE_LORA ** -0.5),
        "k_k": 0.85 + nrm((L, B_WIDTH), 0.02),
        "k_a": gain((L, B_WIDTH)),
        "r_k": nrm((L, B_HEADS, B_HEAD), 0.1),
        "lnx_gain": gain((L, B_WIDTH)),
        "lnx_bias": nrm((L, B_WIDTH), 0.01),
        "conv_w": nrm((L, CONV_K, C_WIDTH), CONV_K ** -0.5),
        "conv_out_norm": gain((L, C_WIDTH)),
        "w_out": nrm((L, D_MIX, D_MODEL), D_MIX ** -0.5),
        "norm_ffn2": gain((L, D_MODEL)),
        "ffn2_gate": nrm((L, D_MODEL, D_FF), D_MODEL ** -0.5),
        "ffn2_up": nrm((L, D_MODEL, D_FF), D_MODEL ** -0.5),
        "ffn2_down": nrm((L, D_FF, D_MODEL), D_FF ** -0.5),
        "norm_final": gain((D_MODEL,)),
    }


def reference(x, positions, norm_ffn1, ffn1_gate, ffn1_up, ffn1_down, norm_mix, w_in,
              q_norm, kv_norm, w_uq, w_ukv, attn_out_norm, shift_mu, decay_w0, decay_up,
              iclr_a0, iclr_up, gate_up, k_k, k_a, r_k, lnx_gain, lnx_bias, conv_w,
              conv_out_norm, w_out, norm_ffn2, ffn2_gate, ffn2_up, ffn2_down, norm_final):
    cos, sin = rope_angles(positions)
    h = x
    for l in range(DEPTH):
        h = h + 0.5 * swiglu(rmsnorm(h, norm_ffn1[l]), ffn1_gate[l], ffn1_up[l], ffn1_down[l])
        p = rmsnorm(h, norm_mix[l]) @ w_in[l]
        pa, pb, pc = p[..., :A_IN], p[..., A_IN:A_IN + B_IN], p[..., A_IN + B_IN:]
        ya = mla_mixer(pa, cos, sin, q_norm[l], kv_norm[l], w_uq[l], w_ukv[l], attn_out_norm[l])
        yb = rwkv7_mixer(pb, shift_mu[l], decay_w0[l], decay_up[l], iclr_a0[l], iclr_up[l],
                         gate_up[l], k_k[l], k_a[l], r_k[l], lnx_gain[l], lnx_bias[l])
        yc = short_conv_mixer(pc, conv_w[l], conv_out_norm[l])
        h = h + jnp.concatenate([ya, yb, yc], axis=-1) @ w_out[l]
        h = h + 0.5 * swiglu(rmsnorm(h, norm_ffn2[l]), ffn2_gate[l], ffn2_up[l], ffn2_down[l])
    return rmsnorm(h, norm_final)
```

```python
import functools

import jax
import jax.numpy as jnp
from jax import lax
from jax.experimental import pallas as pl
from jax.experimental.pallas import tpu as pltpu

F32 = jnp.float32
BF16 = jnp.bfloat16

D_MODEL = 2048
DEPTH = 4
A_WIDTH = 1024
B_WIDTH = 512
C_WIDTH = 512
V_DIM = 128
NOPE_DIM = 128
ROPE_DIM = 64
QK_DIM = NOPE_DIM + ROPE_DIM
A_HEADS = 8
Q_LORA = 512
KV_LORA = 256
ROPE_THETA = 10000.0
B_HEAD = 64
B_HEADS = 8
DECAY_LORA = 32
ICLR_LORA = 32
GATE_LORA = 96
HEAD_NORM_EPS = 64e-5
C_GROUPS = 8
CONV_K = 3
D_FF = 5632
RMS_EPS = 1e-6
A_IN = Q_LORA + KV_LORA + ROPE_DIM
B_IN = 3 * B_WIDTH + DECAY_LORA + ICLR_LORA + GATE_LORA

CKV_W = KV_LORA + 2 * ROPE_DIM
PBM_W = 3 * B_WIDTH
PBL_W = 256
PC_W = 3 * C_WIDTH
P_SPLITS = (Q_LORA, CKV_W, PBM_W, PBL_W, PC_W)
P_TOTAL = sum(P_SPLITS)

V7X_VMEM_BYTES = 64 * 1024 * 1024
SUBLANES = 8
LANES = 128
FFN_TM = 512
FFN_TF = 512
PROJ_TM = 512
SEQ_TILE = 256
ATTN_TQ = 256
CHUNK = 64
CHUNK_SHIFT = CHUNK.bit_length() - 1
INV_BASE_BITS = 3
ROPE_TW = 3 * LANES


def _cparams(sem, vmem_mb):
    return pltpu.CompilerParams(dimension_semantics=sem, vmem_limit_bytes=vmem_mb * 1024 * 1024)


def _rms(x, g, eps=RMS_EPS):
    return x * lax.rsqrt(jnp.mean(x * x, axis=-1, keepdims=True) + eps) * g


def _dot(a, b):
    return jnp.dot(a, b, preferred_element_type=F32)


def _dot_nt(a, b):
    return lax.dot_general(a, b, (((1,), (1,)), ((), ())), preferred_element_type=F32)


def _dot_tn(a, b):
    return lax.dot_general(a, b, (((0,), (0,)), ((), ())), preferred_element_type=F32)


def _split(x):
    hi = x.astype(BF16)
    lo = (x - hi.astype(F32)).astype(BF16)
    return hi, lo


def _dot_exact_rhs(x, m):
    hi, lo = _split(x)
    return _dot(hi, m) + _dot(lo, m)


def _dot3(x, w_hi, w_lo):
    hi, lo = _split(x)
    return _dot(hi, w_hi) + (_dot(lo, w_hi) + _dot(hi, w_lo))


def _shift_rows(prev8, x, k):
    cat = jnp.concatenate([prev8, x], axis=0)
    return pltpu.roll(cat, k, axis=0)[SUBLANES:]


def _group_ones(n, group):
    shift = group.bit_length() - 1
    r = lax.broadcasted_iota(jnp.int32, (n, n), 0) >> shift
    c = lax.broadcasted_iota(jnp.int32, (n, n), 1) >> shift
    return (r == c).astype(BF16)


def _rope_body(pos_ref, inv_ref, o_ref):
    ang = pos_ref[...].astype(F32) * inv_ref[...]
    c = jnp.cos(ang)
    s = jnp.sin(ang)
    lane = lax.broadcasted_iota(jnp.int32, ang.shape, 1)
    ssg = jnp.where((lane & (ROPE_DIM - 1)) < ROPE_DIM // 2, -s, s)
    o_ref[:, 0:LANES] = c
    o_ref[:, LANES:2 * LANES] = ssg
    o_ref[:, 2 * LANES:3 * LANES] = jnp.where(lane < ROPE_DIM, c, ssg)


def _rope_table(pos_col, inv_lane):
    t = pos_col.shape[0]
    tm = 1024
    return pl.pallas_call(
        _rope_body,
        out_shape=jax.ShapeDtypeStruct((t, ROPE_TW), F32),
        grid=(t // tm,),
        in_specs=[pl.BlockSpec((tm, 1), lambda i: (i, 0)),
                  pl.BlockSpec((1, LANES), lambda i: (0, 0))],
        out_specs=pl.BlockSpec((tm, ROPE_TW), lambda i: (i, 0)),
        compiler_params=_cparams(("parallel",), 16),
        name="rope_table",
    )(pos_col, inv_lane)


def _ffn_body(x_ref, g_ref, wg_ref, wu_ref, wd_ref, gf_ref, o_ref, xn_ref, *, final):
    j = pl.program_id(1)

    @pl.when(j == 0)
    def _():
        x = x_ref[...]
        xn_ref[...] = _rms(x, g_ref[...]).astype(BF16)
        o_ref[...] = x

    xn = xn_ref[...]
    gate = _dot(xn, wg_ref[...])
    up = _dot(xn, wu_ref[...])
    act = (gate * jax.nn.sigmoid(gate)) * (up * 0.5)
    o_ref[...] += _dot(act.astype(BF16), wd_ref[...])

    if final:
        @pl.when(j == pl.num_programs(1) - 1)
        def _():
            o_ref[...] = _rms(o_ref[...], gf_ref[...])


def _ffn(h, g, wg, wu, wd, g_final, final):
    t, d = h.shape
    f = wg.shape[1]
    tm, tf = FFN_TM, FFN_TF
    return pl.pallas_call(
        functools.partial(_ffn_body, final=final),
        out_shape=jax.ShapeDtypeStruct((t, d), F32),
        grid=(t // tm, f // tf),
        in_specs=[pl.BlockSpec((tm, d), lambda i, j: (i, 0)),
                  pl.BlockSpec((1, d), lambda i, j: (0, 0)),
                  pl.BlockSpec((d, tf), lambda i, j: (0, j)),
                  pl.BlockSpec((d, tf), lambda i, j: (0, j)),
                  pl.BlockSpec((tf, d), lambda i, j: (j, 0)),
                  pl.BlockSpec((1, d), lambda i, j: (0, 0))],
        out_specs=pl.BlockSpec((tm, d), lambda i, j: (i, 0)),
        scratch_shapes=[pltpu.VMEM((tm, d), BF16)],
        compiler_params=_cparams(("parallel", "arbitrary"), 48),
        name="ffn",
    )(h, g, wg, wu, wd, g_final)


def _inproj_body(x_ref, g_ref, w_ref, cq_ref, ckv_ref, pbm_ref, pbl_ref, pc_ref):
    xn = _rms(x_ref[...], g_ref[...]).astype(BF16)
    off = 0
    for ref in (cq_ref, ckv_ref, pbm_ref, pbl_ref, pc_ref):
        n = ref.shape[-1]
        ref[...] = _dot(xn, w_ref[:, off:off + n])
        off += n


def _in_proj(h, g, w):
    t, d = h.shape
    tm = PROJ_TM
    return pl.pallas_call(
        _inproj_body,
        out_shape=[jax.ShapeDtypeStruct((t, n), F32) for n in P_SPLITS],
        grid=(t // tm,),
        in_specs=[pl.BlockSpec((tm, d), lambda i: (i, 0)),
                  pl.BlockSpec((1, d), lambda i: (0, 0)),
                  pl.BlockSpec((d, P_TOTAL), lambda i: (0, 0), pipeline_mode=pl.Buffered(1))],
        out_specs=[pl.BlockSpec((tm, n), lambda i: (i, 0)) for n in P_SPLITS],
        compiler_params=_cparams(("parallel",), 56),
        name="in_proj",
    )(h, g, w)


def _mla_prep_body(cq_ref, ckv_ref, tab_ref, qn_ref, kvn_ref, wq_ref, wkv_ref, q_ref, k_ref, v_ref):
    nq = A_HEADS * NOPE_DIM
    nr = A_HEADS * ROPE_DIM
    scale = QK_DIM ** -0.5
    tab = tab_ref[0]
    cos2 = tab[:, 0:LANES]
    sin2 = tab[:, LANES:2 * LANES]
    cs = tab[:, 2 * LANES:3 * LANES]

    cqn = _rms(cq_ref[0], qn_ref[...]).astype(BF16)
    qa = _dot(cqn, wq_ref[...])
    cos_h = jnp.concatenate([cos2] * (nr // LANES), axis=1)
    sin_h = jnp.concatenate([sin2] * (nr // LANES), axis=1)
    q_rot = qa[:, nq:nq + nr] * cos_h + qa[:, nq + nr:nq + 2 * nr] * sin_h

    ckv = ckv_ref[0]
    ckvn = _rms(ckv[:, :KV_LORA], kvn_ref[...]).astype(BF16)
    kv = _dot(ckvn, wkv_ref[...])
    t = ckv[:, KV_LORA:KV_LORA + LANES] * cs
    k_rot = (t + pltpu.roll(t, ROPE_DIM, axis=1))[:, :ROPE_DIM].astype(BF16)

    for h in range(A_HEADS):
        q_ref[0, h, :, 0:NOPE_DIM] = (qa[:, h * NOPE_DIM:(h + 1) * NOPE_DIM] * scale).astype(BF16)
        q_ref[0, h, :, NOPE_DIM:QK_DIM] = (q_rot[:, h * ROPE_DIM:(h + 1) * ROPE_DIM] * scale).astype(BF16)
        k_ref[0, h, :, 0:NOPE_DIM] = kv[:, h * NOPE_DIM:(h + 1) * NOPE_DIM].astype(BF16)
        k_ref[0, h, :, NOPE_DIM:QK_DIM] = k_rot
        v_ref[0, h] = kv[:, nq + h * V_DIM:nq + (h + 1) * V_DIM].astype(BF16)


def _mla_prep(cq, ckv, tab, qn, kvn, wq, wkv):
    b, s, _ = cq.shape
    ts = SEQ_TILE
    tok = lambda n: pl.BlockSpec((1, ts, n), lambda bi, i: (bi, i, 0))
    full = lambda a: pl.BlockSpec(a.shape, lambda bi, i: (0, 0))
    head = lambda n: pl.BlockSpec((1, A_HEADS, ts, n), lambda bi, i: (bi, 0, i, 0))
    return pl.pallas_call(
        _mla_prep_body,
        out_shape=[jax.ShapeDtypeStruct((b, A_HEADS, s, QK_DIM), BF16),
                   jax.ShapeDtypeStruct((b, A_HEADS, s, QK_DIM), BF16),
                   jax.ShapeDtypeStruct((b, A_HEADS, s, V_DIM), BF16)],
        grid=(b, s // ts),
        in_specs=[tok(Q_LORA), tok(CKV_W), tok(ROPE_TW), full(qn), full(kvn), full(wq), full(wkv)],
        out_specs=[head(QK_DIM), head(QK_DIM), head(V_DIM)],
        compiler_params=_cparams(("parallel", "parallel"), 32),
        name="mla_prep",
    )(cq, ckv, tab, qn, kvn, wq, wkv)


def _attn_body(q_ref, k_ref, v_ref, g_ref, o_ref):
    i = pl.program_id(2)
    tq = q_ref.shape[2]
    q = q_ref[0, 0]

    def block(j, carry, masked):
        m, l, acc = carry
        start = pl.multiple_of(j * tq, tq)
        kb = k_ref[0, 0, pl.ds(start, tq), :]
        vb = v_ref[0, 0, pl.ds(start, tq), :]
        s = _dot_nt(q, kb)
        if masked:
            row = lax.broadcasted_iota(jnp.int32, s.shape, 0)
            col = lax.broadcasted_iota(jnp.int32, s.shape, 1)
            s = jnp.where(col <= row, s, -1e30)
        m_new = jnp.maximum(m, jnp.max(s, axis=-1, keepdims=True))
        p = jnp.exp(s - m_new)
        alpha = jnp.exp(m - m_new)
        l = alpha * l + jnp.sum(p, axis=-1, keepdims=True)
        acc = alpha * acc + _dot(p.astype(BF16), vb)
        return m_new, l, acc

    init = (jnp.full((tq, 1), -1e30, F32), jnp.zeros((tq, 1), F32), jnp.zeros((tq, V_DIM), F32))
    carry = lax.fori_loop(0, i, lambda j, c: block(j, c, False), init)
    _, l, acc = block(i, carry, True)
    o = acc / l
    o_ref[0] = _rms(o, g_ref[...]).astype(o_ref.dtype)


def _attention(q, k, v, g):
    b, nh, s, _ = q.shape
    tq = ATTN_TQ
    return pl.pallas_call(
        _attn_body,
        out_shape=jax.ShapeDtypeStruct((b, s, nh * V_DIM), BF16),
        grid=(b, nh, s // tq),
        in_specs=[pl.BlockSpec((1, 1, tq, QK_DIM), lambda bi, h, i: (bi, h, i, 0)),
                  pl.BlockSpec((1, 1, s, QK_DIM), lambda bi, h, i: (bi, h, 0, 0)),
                  pl.BlockSpec((1, 1, s, V_DIM), lambda bi, h, i: (bi, h, 0, 0)),
                  pl.BlockSpec((1, V_DIM), lambda bi, h, i: (0, h))],
        out_specs=pl.BlockSpec((1, tq, V_DIM), lambda bi, h, i: (bi, i, h)),
        compiler_params=_cparams(("parallel", "parallel", "arbitrary"), 32),
        name="mla_attention",
    )(q, k, v, g)


def _rwkv_prep_body(pbm_ref, pbl_ref, pbm_prev_ref, pbl_prev_ref, mum_ref, mul_ref, w0_ref, a0_ref,
                    dec_hi_ref, dec_lo_ref, icl_hi_ref, icl_lo_ref, gat_hi_ref, gat_lo_ref,
                    kk_ref, ka_ref, rk_ref,
                    at_ref, rt_ref, bt_ref, kt_ref, bh_ref, kh_ref, v_ref, g_ref, bonus_ref, ptot_ref):
    i = pl.program_id(1)
    ts = pbm_ref.shape[1]
    w = B_WIDTH
    first = i == 0

    xm = pbm_ref[0]
    xl = pbl_ref[0]
    pm = jnp.where(first, 0.0, pbm_prev_ref[0])
    pv = jnp.where(first, 0.0, pbl_prev_ref[0])
    xs = xm + (_shift_rows(pm, xm, 1) - xm) * mum_ref[...]
    ls = xl + (_shift_rows(pv, xl, 1) - xl) * mul_ref[...]
    r = xs[:, 0:w]
    k = xs[:, w:2 * w]
    v = xs[:, 2 * w:3 * w]

    wl = w0_ref[...] + _dot3(jnp.tanh(ls), dec_hi_ref[...], dec_lo_ref[...])
    z = -wl
    softplus = jnp.maximum(z, 0.0) + jnp.log1p(jnp.exp(-jnp.abs(z)))
    lw = -jnp.exp(-softplus - 0.5)
    a_ic = jax.nn.sigmoid(a0_ref[...] + _dot3(ls, icl_hi_ref[...], icl_lo_ref[...]))
    gate = _dot3(jax.nn.sigmoid(ls), gat_hi_ref[...], gat_lo_ref[...])

    seg = _group_ones(w, B_HEAD)
    kk = k * kk_ref[...]
    kkn = kk / jnp.maximum(jnp.sqrt(_dot_exact_rhs(kk * kk, seg)), 1e-12)
    kp = k * (1.0 + (a_ic - 1.0) * ka_ref[...])
    bonus = _dot_exact_rhs(r * kp * rk_ref[...], seg) * v

    row = lax.broadcasted_iota(jnp.int32, (ts, ts), 0)
    col = lax.broadcasted_iota(jnp.int32, (ts, ts), 1)
    same = (row >> CHUNK_SHIFT) == (col >> CHUNK_SHIFT)
    lw_hi, lw_lo = _split(lw)
    tri = jnp.concatenate([(same & (col <= row)).astype(BF16), (same & (col > row)).astype(BF16)], axis=0)
    cums = _dot(tri, lw_hi) + _dot(tri, lw_lo)
    cum = cums[:ts]
    cumr = cums[ts:]

    e_neg = jnp.exp(-cum)
    e_rem = jnp.exp(cumr)
    b_vec = kkn * a_ic
    at_ref[0] = (-kkn * jnp.exp(cum - lw)).astype(BF16)
    rt_ref[0] = (r * jnp.exp(cum)).astype(BF16)
    bt_ref[0] = (b_vec * e_neg).astype(BF16)
    kt_ref[0] = (kp * e_neg).astype(BF16)
    bh_ref[0] = (b_vec * e_rem).astype(BF16)
    kh_ref[0] = (kp * e_rem).astype(BF16)
    v_ref[0] = v.astype(BF16)
    g_ref[0] = gate
    bonus_ref[0] = bonus
    tot = jnp.concatenate([cum[(c + 1) * CHUNK - 1:(c + 1) * CHUNK] for c in range(ts // CHUNK)], axis=0)
    ptot_ref[0, 0] = jnp.exp(tot)


def _rwkv_prep(pbm, pbl, params):
    b, s, _ = pbm.shape
    ts = SEQ_TILE
    nc = ts // CHUNK
    w = B_WIDTH
    tok = lambda n: pl.BlockSpec((1, ts, n), lambda bi, i: (bi, i, 0))
    prev = lambda n: pl.BlockSpec(
        (1, SUBLANES, n), lambda bi, i: (bi, jnp.maximum(i * (ts // SUBLANES) - 1, 0), 0))
    full = lambda a: pl.BlockSpec(a.shape, lambda bi, i: (0, 0))
    return pl.pallas_call(
        _rwkv_prep_body,
        out_shape=[jax.ShapeDtypeStruct((b, s, w), BF16)] * 7
        + [jax.ShapeDtypeStruct((b, s, w), F32)] * 2
        + [jax.ShapeDtypeStruct((b, s // ts, nc, w), F32)],
        grid=(b, s // ts),
        in_specs=[tok(PBM_W), tok(PBL_W), prev(PBM_W), prev(PBL_W)] + [full(p) for p in params],
        out_specs=[tok(w)] * 9 + [pl.BlockSpec((1, 1, nc, w), lambda bi, i: (bi, i, 0, 0))],
        compiler_params=_cparams(("parallel", "parallel"), 40),
        name="rwkv_prep",
    )(pbm, pbl, pbm, pbl, *params)


def _rwkv_scan_body(at_ref, rt_ref, bt_ref, kt_ref, bh_ref, kh_ref, v_ref, g_ref, bonus_ref, ptot_ref,
                    gain_ref, bias_ref, o_ref, state_ref):
    n = B_HEAD
    c_len = CHUNK
    ts = at_ref.shape[1]

    @pl.when(pl.program_id(2) == 0)
    def _():
        state_ref[...] = jnp.zeros_like(state_ref)

    row = lax.broadcasted_iota(jnp.int32, (c_len, 2 * c_len), 0)
    col = lax.broadcasted_iota(jnp.int32, (c_len, 2 * c_len), 1) & (c_len - 1)
    strict2 = col < row
    incl2 = col <= row
    eye = (lax.broadcasted_iota(jnp.int32, (n, n), 0) == lax.broadcasted_iota(jnp.int32, (n, n), 1))
    eye_f = eye.astype(F32)
    zero_blk = jnp.zeros((c_len, n), BF16)
    sq_row = lax.broadcasted_iota(jnp.int32, (c_len, c_len), 0)
    sq_col = lax.broadcasted_iota(jnp.int32, (c_len, c_len), 1)
    same_block = lambda bits: (sq_row >> bits) == (sq_col >> bits)
    diag_blocks = same_block(INV_BASE_BITS)
    merge_masks = [same_block(bits + 1) & ~same_block(bits) for bits in range(INV_BASE_BITS, CHUNK_SHIFT)]

    for c in range(ts // c_len):
        rows = slice(c * c_len, (c + 1) * c_len)
        outs = []
        for h in range(2):
            ln = slice(h * n, (h + 1) * n)
            at = at_ref[0, rows, ln]
            rt = rt_ref[0, rows, ln]
            bt = bt_ref[0, rows, ln]
            kt = kt_ref[0, rows, ln]
            bh = bh_ref[0, rows, ln]
            kh = kh_ref[0, rows, ln]
            vv = v_ref[0, rows, ln]
            s0 = state_ref[h]

            g1 = _dot_nt(jnp.concatenate([at, rt], axis=0), jnp.concatenate([bt, kt], axis=0))
            a_low = jnp.where(strict2, g1[:c_len], 0.0)
            r_low = jnp.where(incl2, g1[c_len:], 0.0).astype(BF16)

            l_mat = a_low[:, :c_len]
            p = jnp.where(diag_blocks, l_mat, 0.0)
            t_inv = eye_f + p
            p = p.astype(BF16)
            for _ in range(INV_BASE_BITS - 1):
                p = _dot(p, p).astype(BF16)
                t_inv = t_inv + _dot(t_inv.astype(BF16), p)
            for off_mask in merge_masks:
                t_b = t_inv.astype(BF16)
                l_off = jnp.where(off_mask, l_mat, 0.0).astype(BF16)
                t_inv = t_inv + _dot(_dot(t_b, l_off).astype(BF16), t_b)

            akv = _dot(a_low[:, c_len:].astype(BF16), vv)
            x = _dot(t_inv.astype(BF16), jnp.concatenate([at, akv.astype(BF16)], axis=1))
            rhs = jnp.concatenate([x.astype(BF16), jnp.concatenate([zero_blk, vv], axis=1)], axis=0)
            qy = _dot(r_low, rhs)
            rp = rt.astype(F32) + qy[:, :n]
            y = _dot_nt(rp.astype(BF16), s0.astype(BF16)) + qy[:, n:]

            wmat = _dot_tn(rhs, jnp.concatenate([bh, kh], axis=0))
            mc = jnp.where(eye, ptot_ref[0, 0, c:c + 1, ln], 0.0) + wmat[:n]
            state_ref[h] = _dot(s0.astype(BF16), mc.astype(BF16)) + wmat[n:]

            mu = jnp.mean(y, axis=-1, keepdims=True)
            yc = y - mu
            var = jnp.mean(yc * yc, axis=-1, keepdims=True)
            yn = yc * lax.rsqrt(var + HEAD_NORM_EPS) * gain_ref[:, ln] + bias_ref[:, ln]
            outs.append((yn + bonus_ref[0, rows, ln]) * g_ref[0, rows, ln])
        o_ref[0, rows, :] = jnp.concatenate(outs, axis=1).astype(o_ref.dtype)


def _rwkv_scan(arrs, gain, bias):
    at = arrs[0]
    b, s, w = at.shape
    ts = SEQ_TILE
    nc = ts // CHUNK
    pair = 2 * B_HEAD
    tok = pl.BlockSpec((1, ts, pair), lambda bi, p, i: (bi, i, p))
    vec = pl.BlockSpec((1, pair), lambda bi, p, i: (0, p))
    return pl.pallas_call(
        _rwkv_scan_body,
        out_shape=jax.ShapeDtypeStruct((b, s, w), BF16),
        grid=(b, w // pair, s // ts),
        in_specs=[tok] * 9 + [pl.BlockSpec((1, 1, nc, pair), lambda bi, p, i: (bi, i, 0, p)), vec, vec],
        out_specs=tok,
        scratch_shapes=[pltpu.VMEM((2, B_HEAD, B_HEAD), F32)],
        compiler_params=_cparams(("parallel", "parallel", "arbitrary"), 32),
        name="rwkv_scan",
    )(*arrs, gain, bias)


def _conv_body(pc_ref, prev_ref, w_ref, g_ref, o_ref):
    w = C_WIDTH
    x = pc_ref[0]
    xp = jnp.where(pl.program_id(1) == 0, 0.0, prev_ref[0])
    u = x[:, w:2 * w] * x[:, 2 * w:3 * w]
    up = xp[:, w:2 * w] * xp[:, 2 * w:3 * w]
    cw = w_ref[...]
    y = cw[0:1] * _shift_rows(up, u, 2) + cw[1:2] * _shift_rows(up, u, 1) + cw[2:3] * u
    z = x[:, 0:w] * y
    ms = _dot_exact_rhs(z * z, _group_ones(w, w // C_GROUPS)) * (C_GROUPS / w)
    o_ref[0] = (z * lax.rsqrt(ms + RMS_EPS) * g_ref[...]).astype(o_ref.dtype)


def _short_conv(pc, conv_w, gain):
    b, s, _ = pc.shape
    ts = SEQ_TILE
    return pl.pallas_call(
        _conv_body,
        out_shape=jax.ShapeDtypeStruct((b, s, C_WIDTH), BF16),
        grid=(b, s // ts),
        in_specs=[pl.BlockSpec((1, ts, PC_W), lambda bi, i: (bi, i, 0)),
                  pl.BlockSpec((1, SUBLANES, PC_W),
                               lambda bi, i: (bi, jnp.maximum(i * (ts // SUBLANES) - 1, 0), 0)),
                  pl.BlockSpec(conv_w.shape, lambda bi, i: (0, 0)),
                  pl.BlockSpec(gain.shape, lambda bi, i: (0, 0))],
        out_specs=pl.BlockSpec((1, ts, C_WIDTH), lambda bi, i: (bi, i, 0)),
        compiler_params=_cparams(("parallel", "parallel"), 32),
        name="short_conv",
    )(pc, pc, conv_w, gain)


def _outproj_body(h_ref, ya_ref, yb_ref, yc_ref, w_ref, o_ref):
    acc = h_ref[...] + _dot(ya_ref[...], w_ref[0:A_WIDTH, :])
    acc = acc + _dot(yb_ref[...], w_ref[A_WIDTH:A_WIDTH + B_WIDTH, :])
    o_ref[...] = acc + _dot(yc_ref[...], w_ref[A_WIDTH + B_WIDTH:, :])


def _out_proj(h, ya, yb, yc, w):
    t, d = h.shape
    tm = PROJ_TM
    row = lambda n: pl.BlockSpec((tm, n), lambda i: (i, 0))
    return pl.pallas_call(
        _outproj_body,
        out_shape=jax.ShapeDtypeStruct((t, d), F32),
        grid=(t // tm,),
        in_specs=[row(d), row(A_WIDTH), row(B_WIDTH), row(C_WIDTH),
                  pl.BlockSpec(w.shape, lambda i: (0, 0))],
        out_specs=row(d),
        compiler_params=_cparams(("parallel",), 48),
        name="out_proj",
    )(h, ya, yb, yc, w)


def _swap_halves(w):
    half = ROPE_DIM // 2
    return jnp.concatenate([w[..., half:], w[..., :half]], axis=-1)


def _arrange_w_in(w_in):
    d = w_in.shape[0]
    o = Q_LORA + KV_LORA
    k_rope = w_in[:, o:A_IN]
    b0 = A_IN
    lora = w_in[:, b0 + PBM_W:b0 + B_IN]
    pad = jnp.zeros((d, PBL_W - lora.shape[1]), w_in.dtype)
    return jnp.concatenate(
        [w_in[:, :Q_LORA], w_in[:, Q_LORA:o], k_rope, _swap_halves(k_rope),
         w_in[:, b0:b0 + PBM_W], lora, pad, w_in[:, b0 + B_IN:]], axis=1).astype(BF16)


def _arrange_w_uq(w_uq):
    w = w_uq.reshape(Q_LORA, A_HEADS, QK_DIM)
    nope = w[:, :, :NOPE_DIM].reshape(Q_LORA, -1)
    rope = w[:, :, NOPE_DIM:]
    return jnp.concatenate(
        [nope, rope.reshape(Q_LORA, -1), _swap_halves(rope).reshape(Q_LORA, -1)], axis=1).astype(BF16)


def _arrange_w_ukv(w_ukv):
    w = w_ukv.reshape(KV_LORA, A_HEADS, NOPE_DIM + V_DIM)
    return jnp.concatenate(
        [w[:, :, :NOPE_DIM].reshape(KV_LORA, -1), w[:, :, NOPE_DIM:].reshape(KV_LORA, -1)], axis=1).astype(BF16)


def _pad_rows(w, start):
    full = jnp.zeros((PBL_W, w.shape[1]), F32).at[start:start + w.shape[0]].set(w)
    hi = full.astype(BF16)
    return hi, (full - hi.astype(F32)).astype(BF16)


def _row(v):
    return v.reshape(1, -1)


def kernel(x, positions, norm_ffn1, ffn1_gate, ffn1_up, ffn1_down, norm_mix, w_in, q_norm, kv_norm, w_uq, w_ukv, attn_out_norm, shift_mu, decay_w0, decay_up, iclr_a0, iclr_up, gate_up, k_k, k_a, r_k, lnx_gain, lnx_bias, conv_w, conv_out_norm, w_out, norm_ffn2, ffn2_gate, ffn2_up, ffn2_down, norm_final):
    b, s, d = x.shape
    t = b * s
    seq = lambda a: a.reshape(b, s, a.shape[-1])

    inv_freq = 1.0 / (ROPE_THETA ** (jnp.arange(0, ROPE_DIM, 2, dtype=F32) / ROPE_DIM))
    inv_lane = jnp.tile(inv_freq, LANES // inv_freq.shape[0]).reshape(1, LANES)
    tab = seq(_rope_table(positions.reshape(t, 1), inv_lane))

    h = x.reshape(t, d)
    g_final = _row(norm_final)
    for l in range(DEPTH):
        h = _ffn(h, _row(norm_ffn1[l]), ffn1_gate[l].astype(BF16), ffn1_up[l].astype(BF16),
                 ffn1_down[l].astype(BF16), g_final, False)

        cq, ckv, pbm, pbl, pc = _in_proj(h, _row(norm_mix[l]), _arrange_w_in(w_in[l]))

        q, k, v = _mla_prep(seq(cq), seq(ckv), tab, _row(q_norm[l]), _row(kv_norm[l]),
                            _arrange_w_uq(w_uq[l]), _arrange_w_ukv(w_ukv[l]))
        ya = _attention(q, k, v, _row(attn_out_norm[l]))

        mu = shift_mu[l]
        mu_l = jnp.zeros((PBL_W,), F32).at[:B_IN - PBM_W].set(mu[PBM_W:])
        rw_params = (_row(mu[:PBM_W]), _row(mu_l), _row(decay_w0[l]), _row(iclr_a0[l]),
                     *_pad_rows(decay_up[l], 0), *_pad_rows(iclr_up[l], DECAY_LORA),
                     *_pad_rows(gate_up[l], DECAY_LORA + ICLR_LORA),
                     _row(k_k[l]), _row(k_a[l]), _row(r_k[l]))
        rw = _rwkv_prep(seq(pbm), seq(pbl), rw_params)
        yb = _rwkv_scan(rw, _row(lnx_gain[l]), _row(lnx_bias[l]))

        yc = _short_conv(seq(pc), conv_w[l], _row(conv_out_norm[l]))

        h = _out_proj(h, ya.reshape(t, A_WIDTH), yb.reshape(t, B_WIDTH), yc.reshape(t, C_WIDTH),
                      w_out[l].astype(BF16))

        h = _ffn(h, _row(norm_ffn2[l]), ffn2_gate[l].astype(BF16), ffn2_up[l].astype(BF16),
                 ffn2_down[l].astype(BF16), g_final, l == DEPTH - 1)
    return h.reshape(b, s, d)
```

```python
import functools

import jax
import jax.numpy as jnp
from jax import lax
from jax.experimental import pallas as pl
from jax.experimental.pallas import tpu as pltpu

F32 = jnp.float32
BF16 = jnp.bfloat16

D_MODEL = 2048
DEPTH = 4
A_WIDTH = 1024
B_WIDTH = 512
C_WIDTH = 512
V_DIM = 128
NOPE_DIM = 128
ROPE_DIM = 64
QK_DIM = NOPE_DIM + ROPE_DIM
A_HEADS = 8
Q_LORA = 512
KV_LORA = 256
ROPE_THETA = 10000.0
B_HEAD = 64
B_HEADS = 8
DECAY_LORA = 32
ICLR_LORA = 32
GATE_LORA = 96
HEAD_NORM_EPS = 64e-5
C_GROUPS = 8
CONV_K = 3
D_FF = 5632
RMS_EPS = 1e-6
A_IN = Q_LORA + KV_LORA + ROPE_DIM
B_IN = 3 * B_WIDTH + DECAY_LORA + ICLR_LORA + GATE_LORA

CKV_W = KV_LORA + 2 * ROPE_DIM
PBM_W = 3 * B_WIDTH
PBL_W = 256
PC_W = 3 * C_WIDTH
P_SPLITS = (Q_LORA, CKV_W, PBM_W, PBL_W, PC_W)
P_TOTAL = sum(P_SPLITS)

V7X_VMEM_BYTES = 64 * 1024 * 1024
SUBLANES = 8
LANES = 128
FFN_TM = 512
FFN_TF = 512
PROJ_TM = 512
SEQ_TILE = 256
ATTN_TQ = 256
CHUNK = 64
CHUNK_SHIFT = CHUNK.bit_length() - 1
INV_BASE_BITS = 3
ROPE_TW = 3 * LANES


def _cparams(sem, vmem_mb):
    return pltpu.CompilerParams(dimension_semantics=sem, vmem_limit_bytes=vmem_mb * 1024 * 1024)


def _rms(x, g, eps=RMS_EPS):
    return x * lax.rsqrt(jnp.mean(x * x, axis=-1, keepdims=True) + eps) * g


def _dot(a, b):
    return jnp.dot(a, b, preferred_element_type=F32)


def _dot_nt(a, b):
    return lax.dot_general(a, b, (((1,), (1,)), ((), ())), preferred_element_type=F32)


def _dot_tn(a, b):
    return lax.dot_general(a, b, (((0,), (0,)), ((), ())), preferred_element_type=F32)


def _split(x):
    hi = x.astype(BF16)
    lo = (x - hi.astype(F32)).astype(BF16)
    return hi, lo


def _dot_exact_rhs(x, m):
    hi, lo = _split(x)
    return _dot(hi, m) + _dot(lo, m)


def _dot3(x, w_hi, w_lo):
    hi, lo = _split(x)
    return _dot(hi, w_hi) + (_dot(lo, w_hi) + _dot(hi, w_lo))


def _shift_rows(prev8, x, k):
    cat = jnp.concatenate([prev8, x], axis=0)
    return pltpu.roll(cat, k, axis=0)[SUBLANES:]


def _group_ones(n, group):
    shift = group.bit_length() - 1
    r = lax.broadcasted_iota(jnp.int32, (n, n), 0) >> shift
    c = lax.broadcasted_iota(jnp.int32, (n, n), 1) >> shift
    return (r == c).astype(BF16)


def _rope_body(pos_ref, inv_ref, o_ref):
    ang = pos_ref[...].astype(F32) * inv_ref[...]
    c = jnp.cos(ang)
    s = jnp.sin(ang)
    lane = lax.broadcasted_iota(jnp.int32, ang.shape, 1)
    ssg = jnp.where((lane & (ROPE_DIM - 1)) < ROPE_DIM // 2, -s, s)
    o_ref[:, 0:LANES] = c
    o_ref[:, LANES:2 * LANES] = ssg
    o_ref[:, 2 * LANES:3 * LANES] = jnp.where(lane < ROPE_DIM, c, ssg)


def _rope_table(pos_col, inv_lane):
    t = pos_col.shape[0]
    tm = 1024
    return pl.pallas_call(
        _rope_body,
        out_shape=jax.ShapeDtypeStruct((t, ROPE_TW), F32),
        grid=(t // tm,),
        in_specs=[pl.BlockSpec((tm, 1), lambda i: (i, 0)),
                  pl.BlockSpec((1, LANES), lambda i: (0, 0))],
        out_specs=pl.BlockSpec((tm, ROPE_TW), lambda i: (i, 0)),
        compiler_params=_cparams(("parallel",), 16),
        name="rope_table",
    )(pos_col, inv_lane)


def _ffn_body(x_ref, g_ref, wg_ref, wu_ref, wd_ref, gf_ref, o_ref, xn_ref, *, final):
    j = pl.program_id(1)

    @pl.when(j == 0)
    def _():
        x = x_ref[...]
        xn_ref[...] = _rms(x, g_ref[...]).astype(BF16)
        o_ref[...] = x

    xn = xn_ref[...]
    gate = _dot(xn, wg_ref[...])
    up = _dot(xn, wu_ref[...])
    act = (gate * jax.nn.sigmoid(gate)) * (up * 0.5)
    o_ref[...] += _dot(act.astype(BF16), wd_ref[...])

    if final:
        @pl.when(j == pl.num_programs(1) - 1)
        def _():
            o_ref[...] = _rms(o_ref[...], gf_ref[...])


def _ffn(h, g, wg, wu, wd, g_final, final):
    t, d = h.shape
    f = wg.shape[1]
    tm, tf = FFN_TM, FFN_TF
    return pl.pallas_call(
        functools.partial(_ffn_body, final=final),
        out_shape=jax.ShapeDtypeStruct((t, d), F32),
        grid=(t // tm, f // tf),
        in_specs=[pl.BlockSpec((tm, d), lambda i, j: (i, 0)),
                  pl.BlockSpec((1, d), lambda i, j: (0, 0)),
                  pl.BlockSpec((d, tf), lambda i, j: (0, j)),
                  pl.BlockSpec((d, tf), lambda i, j: (0, j)),
                  pl.BlockSpec((tf, d), lambda i, j: (j, 0)),
                  pl.BlockSpec((1, d), lambda i, j: (0, 0))],
        out_specs=pl.BlockSpec((tm, d), lambda i, j: (i, 0)),
        scratch_shapes=[pltpu.VMEM((tm, d), BF16)],
        compiler_params=_cparams(("parallel", "arbitrary"), 48),
        name="ffn",
    )(h, g, wg, wu, wd, g_final)


def _inproj_body(x_ref, g_ref, w_ref, cq_ref, ckv_ref, pbm_ref, pbl_ref, pc_ref):
    xn = _rms(x_ref[...], g_ref[...]).astype(BF16)
    off = 0
    for ref in (cq_ref, ckv_ref, pbm_ref, pbl_ref, pc_ref):
        n = ref.shape[-1]
        ref[...] = _dot(xn, w_ref[:, off:off + n])
        off += n


def _in_proj(h, g, w):
    t, d = h.shape
    tm = PROJ_TM
    return pl.pallas_call(
        _inproj_body,
        out_shape=[jax.ShapeDtypeStruct((t, n), F32) for n in P_SPLITS],
        grid=(t // tm,),
        in_specs=[pl.BlockSpec((tm, d), lambda i: (i, 0)),
                  pl.BlockSpec((1, d), lambda i: (0, 0)),
                  pl.BlockSpec((d, P_TOTAL), lambda i: (0, 0), pipeline_mode=pl.Buffered(1))],
        out_specs=[pl.BlockSpec((tm, n), lambda i: (i, 0)) for n in P_SPLITS],
        compiler_params=_cparams(("parallel",), 56),
        name="in_proj",
    )(h, g, w)


def _mla_prep_body(cq_ref, ckv_ref, tab_ref, qn_ref, kvn_ref, wq_ref, wkv_ref, q_ref, k_ref, v_ref):
    nq = A_HEADS * NOPE_DIM
    nr = A_HEADS * ROPE_DIM
    scale = QK_DIM ** -0.5
    tab = tab_ref[0]
    cos2 = tab[:, 0:LANES]
    sin2 = tab[:, LANES:2 * LANES]
    cs = tab[:, 2 * LANES:3 * LANES]

    cqn = _rms(cq_ref[0], qn_ref[...]).astype(BF16)
    qa = _dot(cqn, wq_ref[...])
    cos_h = jnp.concatenate([cos2] * (nr // LANES), axis=1)
    sin_h = jnp.concatenate([sin2] * (nr // LANES), axis=1)
    q_rot = qa[:, nq:nq + nr] * cos_h + qa[:, nq + nr:nq + 2 * nr] * sin_h

    ckv = ckv_ref[0]
    ckvn = _rms(ckv[:, :KV_LORA], kvn_ref[...]).astype(BF16)
    kv = _dot(ckvn, wkv_ref[...])
    t = ckv[:, KV_LORA:KV_LORA + LANES] * cs
    k_rot = (t + pltpu.roll(t, ROPE_DIM, axis=1))[:, :ROPE_DIM].astype(BF16)

    for h in range(A_HEADS):
        q_ref[0, h, :, 0:NOPE_DIM] = (qa[:, h * NOPE_DIM:(h + 1) * NOPE_DIM] * scale).astype(BF16)
        q_ref[0, h, :, NOPE_DIM:QK_DIM] = (q_rot[:, h * ROPE_DIM:(h + 1) * ROPE_DIM] * scale).astype(BF16)
        k_ref[0, h, :, 0:NOPE_DIM] = kv[:, h * NOPE_DIM:(h + 1) * NOPE_DIM].astype(BF16)
        k_ref[0, h, :, NOPE_DIM:QK_DIM] = k_rot
        v_ref[0, h] = kv[:, nq + h * V_DIM:nq + (h + 1) * V_DIM].astype(BF16)


def _mla_prep(cq, ckv, tab, qn, kvn, wq, wkv):
    b, s, _ = cq.shape
    ts = SEQ_TILE
    tok = lambda n: pl.BlockSpec((1, ts, n), lambda bi, i: (bi, i, 0))
    full = lambda a: pl.BlockSpec(a.shape, lambda bi, i: (0, 0))
    head = lambda n: pl.BlockSpec((1, A_HEADS, ts, n), lambda bi, i: (bi, 0, i, 0))
    return pl.pallas_call(
        _mla_prep_body,
        out_shape=[jax.ShapeDtypeStruct((b, A_HEADS, s, QK_DIM), BF16),
                   jax.ShapeDtypeStruct((b, A_HEADS, s, QK_DIM), BF16),
                   jax.ShapeDtypeStruct((b, A_HEADS, s, V_DIM), BF16)],
        grid=(b, s // ts),
        in_specs=[tok(Q_LORA), tok(CKV_W), tok(ROPE_TW), full(qn), full(kvn), full(wq), full(wkv)],
        out_specs=[head(QK_DIM), head(QK_DIM), head(V_DIM)],
        compiler_params=_cparams(("parallel", "parallel"), 32),
        name="mla_prep",
    )(cq, ckv, tab, qn, kvn, wq, wkv)


def _attn_body(q_ref, k_ref, v_ref, g_ref, o_ref):
    i = pl.program_id(1)
    nh = q_ref.shape[1]
    tq = q_ref.shape[2]

    def block(j, carry, masked):
        start = pl.multiple_of(j * tq, tq)
        out = []
        for h in range(nh):
            m, l, acc = carry[h]
            s = _dot_nt(q_ref[0, h], k_ref[0, h, pl.ds(start, tq), :])
            if masked:
                row = lax.broadcasted_iota(jnp.int32, s.shape, 0)
                col = lax.broadcasted_iota(jnp.int32, s.shape, 1)
                s = jnp.where(col <= row, s, -1e30)
            m_new = jnp.maximum(m, jnp.max(s, axis=-1, keepdims=True))
            p = jnp.exp(s - m_new)
            alpha = jnp.exp(m - m_new)
            l = alpha * l + jnp.sum(p, axis=-1, keepdims=True)
            acc = alpha * acc + _dot(p.astype(BF16), v_ref[0, h, pl.ds(start, tq), :])
            out.append((m_new, l, acc))
        return tuple(out)

    init = tuple((jnp.full((tq, 1), -1e30, F32), jnp.zeros((tq, 1), F32), jnp.zeros((tq, V_DIM), F32))
                 for _ in range(nh))
    carry = lax.fori_loop(0, i, lambda j, c: block(j, c, False), init)
    carry = block(i, carry, True)
    for h in range(nh):
        _, l, acc = carry[h]
        hs = slice(h * V_DIM, (h + 1) * V_DIM)
        o_ref[0, :, hs] = _rms(acc / l, g_ref[:, hs]).astype(o_ref.dtype)


def _attention(q, k, v, g):
    b, nh, s, _ = q.shape
    tq = ATTN_TQ
    return pl.pallas_call(
        _attn_body,
        out_shape=jax.ShapeDtypeStruct((b, s, nh * V_DIM), BF16),
        grid=(b, s // tq),
        in_specs=[pl.BlockSpec((1, nh, tq, QK_DIM), lambda bi, i: (bi, 0, i, 0)),
                  pl.BlockSpec((1, nh, s, QK_DIM), lambda bi, i: (bi, 0, 0, 0)),
                  pl.BlockSpec((1, nh, s, V_DIM), lambda bi, i: (bi, 0, 0, 0)),
                  pl.BlockSpec((1, nh * V_DIM), lambda bi, i: (0, 0))],
        out_specs=pl.BlockSpec((1, tq, nh * V_DIM), lambda bi, i: (bi, i, 0)),
        compiler_params=_cparams(("parallel", "arbitrary"), 48),
        name="mla_attention",
    )(q, k, v, g)


def _rwkv_prep_body(pbm_ref, pbl_ref, pbm_prev_ref, pbl_prev_ref, mum_ref, mul_ref, w0_ref, a0_ref,
                    dec_hi_ref, dec_lo_ref, icl_hi_ref, icl_lo_ref, gat_hi_ref, gat_lo_ref,
                    kk_ref, ka_ref, rk_ref,
                    at_ref, rt_ref, bt_ref, kt_ref, bh_ref, kh_ref, v_ref, g_ref, bonus_ref, ptot_ref):
    i = pl.program_id(1)
    ts = pbm_ref.shape[1]
    w = B_WIDTH
    first = i == 0

    xm = pbm_ref[0]
    xl = pbl_ref[0]
    pm = jnp.where(first, 0.0, pbm_prev_ref[0])
    pv = jnp.where(first, 0.0, pbl_prev_ref[0])
    xs = xm + (_shift_rows(pm, xm, 1) - xm) * mum_ref[...]
    ls = xl + (_shift_rows(pv, xl, 1) - xl) * mul_ref[...]
    r = xs[:, 0:w]
    k = xs[:, w:2 * w]
    v = xs[:, 2 * w:3 * w]

    wl = w0_ref[...] + _dot3(jnp.tanh(ls), dec_hi_ref[...], dec_lo_ref[...])
    z = -wl
    softplus = jnp.maximum(z, 0.0) + jnp.log1p(jnp.exp(-jnp.abs(z)))
    lw = -jnp.exp(-softplus - 0.5)
    a_ic = jax.nn.sigmoid(a0_ref[...] + _dot3(ls, icl_hi_ref[...], icl_lo_ref[...]))
    gate = _dot3(jax.nn.sigmoid(ls), gat_hi_ref[...], gat_lo_ref[...])

    seg = _group_ones(w, B_HEAD)
    kk = k * kk_ref[...]
    kkn = kk / jnp.maximum(jnp.sqrt(_dot_exact_rhs(kk * kk, seg)), 1e-12)
    kp = k * (1.0 + (a_ic - 1.0) * ka_ref[...])
    bonus = _dot_exact_rhs(r * kp * rk_ref[...], seg) * v

    row = lax.broadcasted_iota(jnp.int32, (ts, ts), 0)
    col = lax.broadcasted_iota(jnp.int32, (ts, ts), 1)
    same = (row >> CHUNK_SHIFT) == (col >> CHUNK_SHIFT)
    lw_hi, lw_lo = _split(lw)
    tri = jnp.concatenate([(same & (col <= row)).astype(BF16), (same & (col > row)).astype(BF16)], axis=0)
    cums = _dot(tri, lw_hi) + _dot(tri, lw_lo)
    cum = cums[:ts]
    cumr = cums[ts:]

    e_neg = jnp.exp(-cum)
    e_rem = jnp.exp(cumr)
    b_vec = kkn * a_ic
    at_ref[0] = (-kkn * jnp.exp(cum - lw)).astype(BF16)
    rt_ref[0] = (r * jnp.exp(cum)).astype(BF16)
    bt_ref[0] = (b_vec * e_neg).astype(BF16)
    kt_ref[0] = (kp * e_neg).astype(BF16)
    bh_ref[0] = (b_vec * e_rem).astype(BF16)
    kh_ref[0] = (kp * e_rem).astype(BF16)
    v_ref[0] = v.astype(BF16)
    g_ref[0] = gate
    bonus_ref[0] = bonus
    tot = jnp.concatenate([cum[(c + 1) * CHUNK - 1:(c + 1) * CHUNK] for c in range(ts // CHUNK)], axis=0)
    ptot_ref[0, 0] = jnp.exp(tot)


def _rwkv_prep(pbm, pbl, params):
    b, s, _ = pbm.shape
    ts = SEQ_TILE
    nc = ts // CHUNK
    w = B_WIDTH
    tok = lambda n: pl.BlockSpec((1, ts, n), lambda bi, i: (bi, i, 0))
    prev = lambda n: pl.BlockSpec(
        (1, SUBLANES, n), lambda bi, i: (bi, jnp.maximum(i * (ts // SUBLANES) - 1, 0), 0))
    full = lambda a: pl.BlockSpec(a.shape, lambda bi, i: (0, 0))
    return pl.pallas_call(
        _rwkv_prep_body,
        out_shape=[jax.ShapeDtypeStruct((b, s, w), BF16)] * 7
        + [jax.ShapeDtypeStruct((b, s, w), F32)] * 2
        + [jax.ShapeDtypeStruct((b, s // ts, nc, w), F32)],
        grid=(b, s // ts),
        in_specs=[tok(PBM_W), tok(PBL_W), prev(PBM_W), prev(PBL_W)] + [full(p) for p in params],
        out_specs=[tok(w)] * 9 + [pl.BlockSpec((1, 1, nc, w), lambda bi, i: (bi, i, 0, 0))],
        compiler_params=_cparams(("parallel", "parallel"), 40),
        name="rwkv_prep",
    )(pbm, pbl, pbm, pbl, *params)


def _rwkv_scan_body(at_ref, rt_ref, bt_ref, kt_ref, bh_ref, kh_ref, v_ref, g_ref, bonus_ref, ptot_ref,
                    gain_ref, bias_ref, o_ref, state_ref):
    n = B_HEAD
    c_len = CHUNK
    ts = at_ref.shape[1]

    @pl.when(pl.program_id(2) == 0)
    def _():
        state_ref[...] = jnp.zeros_like(state_ref)

    row = lax.broadcasted_iota(jnp.int32, (c_len, 2 * c_len), 0)
    col = lax.broadcasted_iota(jnp.int32, (c_len, 2 * c_len), 1) & (c_len - 1)
    strict2 = col < row
    incl2 = col <= row
    eye = (lax.broadcasted_iota(jnp.int32, (n, n), 0) == lax.broadcasted_iota(jnp.int32, (n, n), 1))
    eye_f = eye.astype(F32)
    zero_blk = jnp.zeros((c_len, n), BF16)
    sq_row = lax.broadcasted_iota(jnp.int32, (c_len, c_len), 0)
    sq_col = lax.broadcasted_iota(jnp.int32, (c_len, c_len), 1)
    same_block = lambda bits: (sq_row >> bits) == (sq_col >> bits)
    diag_blocks = same_block(INV_BASE_BITS)
    merge_masks = [same_block(bits + 1) & ~same_block(bits) for bits in range(INV_BASE_BITS, CHUNK_SHIFT)]

    n_chunks = ts // c_len
    units = [(c, h) for c in range(n_chunks) for h in range(2)]
    rows = lambda c: slice(c * c_len, (c + 1) * c_len)
    lanes = lambda h: slice(h * n, (h + 1) * n)
    load = lambda ref: {(c, h): ref[0, rows(c), lanes(h)] for c, h in units}
    at, rt, bt, kt, bh, kh, vv = (load(r) for r in (at_ref, rt_ref, bt_ref, kt_ref, bh_ref, kh_ref, v_ref))
    state = [state_ref[h] for h in range(2)]

    g1 = {u: _dot_nt(jnp.concatenate([at[u], rt[u]], axis=0), jnp.concatenate([bt[u], kt[u]], axis=0))
          for u in units}
    a_low = {u: jnp.where(strict2, g1[u][:c_len], 0.0) for u in units}
    r_low = {u: jnp.where(incl2, g1[u][c_len:], 0.0).astype(BF16) for u in units}

    l_mat = {u: a_low[u][:, :c_len] for u in units}
    p = {u: jnp.where(diag_blocks, l_mat[u], 0.0) for u in units}
    t_inv = {u: eye_f + p[u] for u in units}
    p = {u: p[u].astype(BF16) for u in units}
    for _ in range(INV_BASE_BITS - 1):
        p = {u: _dot(p[u], p[u]).astype(BF16) for u in units}
        t_inv = {u: t_inv[u] + _dot(t_inv[u].astype(BF16), p[u]) for u in units}
    for off_mask in merge_masks:
        t_b = {u: t_inv[u].astype(BF16) for u in units}
        t_l = {u: _dot(t_b[u], jnp.where(off_mask, l_mat[u], 0.0).astype(BF16)).astype(BF16) for u in units}
        t_inv = {u: t_inv[u] + _dot(t_l[u], t_b[u]) for u in units}

    akv = {u: _dot(a_low[u][:, c_len:].astype(BF16), vv[u]).astype(BF16) for u in units}
    x = {u: _dot(t_inv[u].astype(BF16), jnp.concatenate([at[u], akv[u]], axis=1)).astype(BF16) for u in units}
    rhs = {u: jnp.concatenate([x[u], jnp.concatenate([zero_blk, vv[u]], axis=1)], axis=0) for u in units}
    qy = {u: _dot(r_low[u], rhs[u]) for u in units}
    wmat = {u: _dot_tn(rhs[u], jnp.concatenate([bh[u], kh[u]], axis=0)) for u in units}
    mc = {(c, h): (jnp.where(eye, ptot_ref[0, 0, c:c + 1, lanes(h)], 0.0) + wmat[(c, h)][:n]).astype(BF16)
          for c, h in units}
    rp = {u: (rt[u].astype(F32) + qy[u][:, :n]).astype(BF16) for u in units}

    y = {}
    for c, h in units:
        s0 = state[h].astype(BF16)
        y[(c, h)] = _dot_nt(rp[(c, h)], s0) + qy[(c, h)][:, n:]
        state[h] = _dot(s0, mc[(c, h)]) + wmat[(c, h)][n:]

    for c in range(n_chunks):
        outs = []
        for h in range(2):
            yv = y[(c, h)]
            mu = jnp.mean(yv, axis=-1, keepdims=True)
            yc = yv - mu
            var = jnp.mean(yc * yc, axis=-1, keepdims=True)
            yn = yc * lax.rsqrt(var + HEAD_NORM_EPS) * gain_ref[:, lanes(h)] + bias_ref[:, lanes(h)]
            outs.append((yn + bonus_ref[0, rows(c), lanes(h)]) * g_ref[0, rows(c), lanes(h)])
        o_ref[0, rows(c), :] = jnp.concatenate(outs, axis=1).astype(o_ref.dtype)
    for h in range(2):
        state_ref[h] = state[h]


def _rwkv_scan(arrs, gain, bias):
    at = arrs[0]
    b, s, w = at.shape
    ts = SEQ_TILE
    nc = ts // CHUNK
    pair = 2 * B_HEAD
    tok = pl.BlockSpec((1, ts, pair), lambda bi, p, i: (bi, i, p))
    vec = pl.BlockSpec((1, pair), lambda bi, p, i: (0, p))
    return pl.pallas_call(
        _rwkv_scan_body,
        out_shape=jax.ShapeDtypeStruct((b, s, w), BF16),
        grid=(b, w // pair, s // ts),
        in_specs=[tok] * 9 + [pl.BlockSpec((1, 1, nc, pair), lambda bi, p, i: (bi, i, 0, p)), vec, vec],
        out_specs=tok,
        scratch_shapes=[pltpu.VMEM((2, B_HEAD, B_HEAD), F32)],
        compiler_params=_cparams(("parallel", "parallel", "arbitrary"), 32),
        name="rwkv_scan",
    )(*arrs, gain, bias)


def _conv_body(pc_ref, prev_ref, w_ref, g_ref, o_ref):
    w = C_WIDTH
    x = pc_ref[0]
    xp = jnp.where(pl.program_id(1) == 0, 0.0, prev_ref[0])
    u = x[:, w:2 * w] * x[:, 2 * w:3 * w]
    up = xp[:, w:2 * w] * xp[:, 2 * w:3 * w]
    cw = w_ref[...]
    y = cw[0:1] * _shift_rows(up, u, 2) + cw[1:2] * _shift_rows(up, u, 1) + cw[2:3] * u
    z = x[:, 0:w] * y
    ms = _dot_exact_rhs(z * z, _group_ones(w, w // C_GROUPS)) * (C_GROUPS / w)
    o_ref[0] = (z * lax.rsqrt(ms + RMS_EPS) * g_ref[...]).astype(o_ref.dtype)


def _short_conv(pc, conv_w, gain):
    b, s, _ = pc.shape
    ts = SEQ_TILE
    return pl.pallas_call(
        _conv_body,
        out_shape=jax.ShapeDtypeStruct((b, s, C_WIDTH), BF16),
        grid=(b, s // ts),
        in_specs=[pl.BlockSpec((1, ts, PC_W), lambda bi, i: (bi, i, 0)),
                  pl.BlockSpec((1, SUBLANES, PC_W),
                               lambda bi, i: (bi, jnp.maximum(i * (ts // SUBLANES) - 1, 0), 0)),
                  pl.BlockSpec(conv_w.shape, lambda bi, i: (0, 0)),
                  pl.BlockSpec(gain.shape, lambda bi, i: (0, 0))],
        out_specs=pl.BlockSpec((1, ts, C_WIDTH), lambda bi, i: (bi, i, 0)),
        compiler_params=_cparams(("parallel", "parallel"), 32),
        name="short_conv",
    )(pc, pc, conv_w, gain)


def _outproj_body(h_ref, ya_ref, yb_ref, yc_ref, w_ref, o_ref):
    acc = h_ref[...] + _dot(ya_ref[...], w_ref[0:A_WIDTH, :])
    acc = acc + _dot(yb_ref[...], w_ref[A_WIDTH:A_WIDTH + B_WIDTH, :])
    o_ref[...] = acc + _dot(yc_ref[...], w_ref[A_WIDTH + B_WIDTH:, :])


def _out_proj(h, ya, yb, yc, w):
    t, d = h.shape
    tm = PROJ_TM
    row = lambda n: pl.BlockSpec((tm, n), lambda i: (i, 0))
    return pl.pallas_call(
        _outproj_body,
        out_shape=jax.ShapeDtypeStruct((t, d), F32),
        grid=(t // tm,),
        in_specs=[row(d), row(A_WIDTH), row(B_WIDTH), row(C_WIDTH),
                  pl.BlockSpec(w.shape, lambda i: (0, 0))],
        out_specs=row(d),
        compiler_params=_cparams(("parallel",), 48),
        name="out_proj",
    )(h, ya, yb, yc, w)


def _swap_halves(w):
    half = ROPE_DIM // 2
    return jnp.concatenate([w[..., half:], w[..., :half]], axis=-1)


def _arrange_w_in(w_in):
    d = w_in.shape[0]
    o = Q_LORA + KV_LORA
    k_rope = w_in[:, o:A_IN]
    b0 = A_IN
    lora = w_in[:, b0 + PBM_W:b0 + B_IN]
    pad = jnp.zeros((d, PBL_W - lora.shape[1]), w_in.dtype)
    return jnp.concatenate(
        [w_in[:, :Q_LORA], w_in[:, Q_LORA:o], k_rope, _swap_halves(k_rope),
         w_in[:, b0:b0 + PBM_W], lora, pad, w_in[:, b0 + B_IN:]], axis=1).astype(BF16)


def _arrange_w_uq(w_uq):
    w = w_uq.reshape(Q_LORA, A_HEADS, QK_DIM)
    nope = w[:, :, :NOPE_DIM].reshape(Q_LORA, -1)
    rope = w[:, :, NOPE_DIM:]
    return jnp.concatenate(
        [nope, rope.reshape(Q_LORA, -1), _swap_halves(rope).reshape(Q_LORA, -1)], axis=1).astype(BF16)


def _arrange_w_ukv(w_ukv):
    w = w_ukv.reshape(KV_LORA, A_HEADS, NOPE_DIM + V_DIM)
    return jnp.concatenate(
        [w[:, :, :NOPE_DIM].reshape(KV_LORA, -1), w[:, :, NOPE_DIM:].reshape(KV_LORA, -1)], axis=1).astype(BF16)


def _pad_rows(w, start):
    full = jnp.zeros((PBL_W, w.shape[1]), F32).at[start:start + w.shape[0]].set(w)
    hi = full.astype(BF16)
    return hi, (full - hi.astype(F32)).astype(BF16)


def _row(v):
    return v.reshape(1, -1)


def kernel(x, positions, norm_ffn1, ffn1_gate, ffn1_up, ffn1_down, norm_mix, w_in, q_norm, kv_norm, w_uq, w_ukv, attn_out_norm, shift_mu, decay_w0, decay_up, iclr_a0, iclr_up, gate_up, k_k, k_a, r_k, lnx_gain, lnx_bias, conv_w, conv_out_norm, w_out, norm_ffn2, ffn2_gate, ffn2_up, ffn2_down, norm_final):
    b, s, d = x.shape
    t = b * s
    seq = lambda a: a.reshape(b, s, a.shape[-1])

    inv_freq = 1.0 / (ROPE_THETA ** (jnp.arange(0, ROPE_DIM, 2, dtype=F32) / ROPE_DIM))
    inv_lane = jnp.tile(inv_freq, LANES // inv_freq.shape[0]).reshape(1, LANES)
    tab = seq(_rope_table(positions.reshape(t, 1), inv_lane))

    h = x.reshape(t, d)
    g_final = _row(norm_final)
    for l in range(DEPTH):
        h = _ffn(h, _row(norm_ffn1[l]), ffn1_gate[l].astype(BF16), ffn1_up[l].astype(BF16),
                 ffn1_down[l].astype(BF16), g_final, False)

        cq, ckv, pbm, pbl, pc = _in_proj(h, _row(norm_mix[l]), _arrange_w_in(w_in[l]))

        q, k, v = _mla_prep(seq(cq), seq(ckv), tab, _row(q_norm[l]), _row(kv_norm[l]),
                            _arrange_w_uq(w_uq[l]), _arrange_w_ukv(w_ukv[l]))
        ya = _attention(q, k, v, _row(attn_out_norm[l]))

        mu = shift_mu[l]
        mu_l = jnp.zeros((PBL_W,), F32).at[:B_IN - PBM_W].set(mu[PBM_W:])
        rw_params = (_row(mu[:PBM_W]), _row(mu_l), _row(decay_w0[l]), _row(iclr_a0[l]),
                     *_pad_rows(decay_up[l], 0), *_pad_rows(iclr_up[l], DECAY_LORA),
                     *_pad_rows(gate_up[l], DECAY_LORA + ICLR_LORA),
                     _row(k_k[l]), _row(k_a[l]), _row(r_k[l]))
        rw = _rwkv_prep(seq(pbm), seq(pbl), rw_params)
        yb = _rwkv_scan(rw, _row(lnx_gain[l]), _row(lnx_bias[l]))

        yc = _short_conv(seq(pc), conv_w[l], _row(conv_out_norm[l]))

        h = _out_proj(h, ya.reshape(t, A_WIDTH), yb.reshape(t, B_WIDTH), yc.reshape(t, C_WIDTH),
                      w_out[l].astype(BF16))

        h = _ffn(h, _row(norm_ffn2[l]), ffn2_gate[l].astype(BF16), ffn2_up[l].astype(BF16),
                 ffn2_down[l].astype(BF16), g_final, l == DEPTH - 1)
    return h.reshape(b, s, d)
```

```python
import functools

import jax
import jax.numpy as jnp
from jax import lax
from jax.experimental import pallas as pl
from jax.experimental.pallas import tpu as pltpu

F32 = jnp.float32
BF16 = jnp.bfloat16

D_MODEL = 2048
DEPTH = 4
A_WIDTH = 1024
B_WIDTH = 512
C_WIDTH = 512
V_DIM = 128
NOPE_DIM = 128
ROPE_DIM = 64
QK_DIM = NOPE_DIM + ROPE_DIM
A_HEADS = 8
Q_LORA = 512
KV_LORA = 256
ROPE_THETA = 10000.0
B_HEAD = 64
B_HEADS = 8
DECAY_LORA = 32
ICLR_LORA = 32
GATE_LORA = 96
HEAD_NORM_EPS = 64e-5
C_GROUPS = 8
CONV_K = 3
D_FF = 5632
RMS_EPS = 1e-6
A_IN = Q_LORA + KV_LORA + ROPE_DIM
B_IN = 3 * B_WIDTH + DECAY_LORA + ICLR_LORA + GATE_LORA

CKV_W = KV_LORA + 2 * ROPE_DIM
PBM_W = 3 * B_WIDTH
PBL_W = 256
PC_W = 3 * C_WIDTH
P_SPLITS = (Q_LORA, CKV_W, PBM_W, PBL_W, PC_W)
P_TOTAL = sum(P_SPLITS)

V7X_VMEM_BYTES = 64 * 1024 * 1024
SUBLANES = 8
LANES = 128
FFN_TM = 512
FFN_TF = 512
PROJ_TM = 512
SEQ_TILE = 256
ATTN_TQ = 256
CHUNK = 64
CHUNK_SHIFT = CHUNK.bit_length() - 1
INV_BASE_BITS = 3
ROPE_TW = 3 * LANES


def _cparams(sem, vmem_mb):
    return pltpu.CompilerParams(dimension_semantics=sem, vmem_limit_bytes=vmem_mb * 1024 * 1024)


def _rms(x, g, eps=RMS_EPS):
    return x * lax.rsqrt(jnp.mean(x * x, axis=-1, keepdims=True) + eps) * g


def _dot(a, b):
    return jnp.dot(a, b, preferred_element_type=F32)


def _dot_nt(a, b):
    return lax.dot_general(a, b, (((1,), (1,)), ((), ())), preferred_element_type=F32)


def _dot_tn(a, b):
    return lax.dot_general(a, b, (((0,), (0,)), ((), ())), preferred_element_type=F32)


def _split(x):
    hi = x.astype(BF16)
    lo = (x - hi.astype(F32)).astype(BF16)
    return hi, lo


def _dot_exact_rhs(x, m):
    hi, lo = _split(x)
    return _dot(hi, m) + _dot(lo, m)


def _dot3(x, w_hi, w_lo):
    hi, lo = _split(x)
    return _dot(hi, w_hi) + (_dot(lo, w_hi) + _dot(hi, w_lo))


def _shift_rows(prev8, x, k):
    cat = jnp.concatenate([prev8, x], axis=0)
    return pltpu.roll(cat, k, axis=0)[SUBLANES:]


def _group_ones(n, group):
    shift = group.bit_length() - 1
    r = lax.broadcasted_iota(jnp.int32, (n, n), 0) >> shift
    c = lax.broadcasted_iota(jnp.int32, (n, n), 1) >> shift
    return (r == c).astype(BF16)


def _rope_body(pos_ref, inv_ref, o_ref):
    ang = pos_ref[...].astype(F32) * inv_ref[...]
    c = jnp.cos(ang)
    s = jnp.sin(ang)
    lane = lax.broadcasted_iota(jnp.int32, ang.shape, 1)
    ssg = jnp.where((lane & (ROPE_DIM - 1)) < ROPE_DIM // 2, -s, s)
    o_ref[:, 0:LANES] = c
    o_ref[:, LANES:2 * LANES] = ssg
    o_ref[:, 2 * LANES:3 * LANES] = jnp.where(lane < ROPE_DIM, c, ssg)


def _rope_table(pos_col, inv_lane):
    t = pos_col.shape[0]
    tm = 1024
    return pl.pallas_call(
        _rope_body,
        out_shape=jax.ShapeDtypeStruct((t, ROPE_TW), F32),
        grid=(t // tm,),
        in_specs=[pl.BlockSpec((tm, 1), lambda i: (i, 0)),
                  pl.BlockSpec((1, LANES), lambda i: (0, 0))],
        out_specs=pl.BlockSpec((tm, ROPE_TW), lambda i: (i, 0)),
        compiler_params=_cparams(("parallel",), 16),
        name="rope_table",
    )(pos_col, inv_lane)


def _ffn_body(x_ref, g_ref, wg_ref, wu_ref, wd_ref, gf_ref, o_ref, xn_ref, *, final):
    j = pl.program_id(1)

    @pl.when(j == 0)
    def _():
        x = x_ref[...]
        xn_ref[...] = _rms(x, g_ref[...]).astype(BF16)
        o_ref[...] = x

    xn = xn_ref[...]
    gate = _dot(xn, wg_ref[...])
    up = _dot(xn, wu_ref[...])
    act = (gate * jax.nn.sigmoid(gate)) * (up * 0.5)
    o_ref[...] += _dot(act.astype(BF16), wd_ref[...])

    if final:
        @pl.when(j == pl.num_programs(1) - 1)
        def _():
            o_ref[...] = _rms(o_ref[...], gf_ref[...])


def _ffn(h, g, wg, wu, wd, g_final, final):
    t, d = h.shape
    f = wg.shape[1]
    tm, tf = FFN_TM, FFN_TF
    return pl.pallas_call(
        functools.partial(_ffn_body, final=final),
        out_shape=jax.ShapeDtypeStruct((t, d), F32),
        grid=(t // tm, f // tf),
        in_specs=[pl.BlockSpec((tm, d), lambda i, j: (i, 0)),
                  pl.BlockSpec((1, d), lambda i, j: (0, 0)),
                  pl.BlockSpec((d, tf), lambda i, j: (0, j)),
                  pl.BlockSpec((d, tf), lambda i, j: (0, j)),
                  pl.BlockSpec((tf, d), lambda i, j: (j, 0)),
                  pl.BlockSpec((1, d), lambda i, j: (0, 0))],
        out_specs=pl.BlockSpec((tm, d), lambda i, j: (i, 0)),
        scratch_shapes=[pltpu.VMEM((tm, d), BF16)],
        compiler_params=_cparams(("parallel", "arbitrary"), 48),
        name="ffn",
    )(h, g, wg, wu, wd, g_final)


def _inproj_body(x_ref, g_ref, w_ref, cq_ref, ckv_ref, pbm_ref, pbl_ref, pc_ref):
    xn = _rms(x_ref[...], g_ref[...]).astype(BF16)
    off = 0
    for ref in (cq_ref, ckv_ref, pbm_ref, pbl_ref, pc_ref):
        n = ref.shape[-1]
        ref[...] = _dot(xn, w_ref[:, off:off + n])
        off += n


def _in_proj(h, g, w):
    t, d = h.shape
    tm = PROJ_TM
    return pl.pallas_call(
        _inproj_body,
        out_shape=[jax.ShapeDtypeStruct((t, n), F32) for n in P_SPLITS],
        grid=(t // tm,),
        in_specs=[pl.BlockSpec((tm, d), lambda i: (i, 0)),
                  pl.BlockSpec((1, d), lambda i: (0, 0)),
                  pl.BlockSpec((d, P_TOTAL), lambda i: (0, 0), pipeline_mode=pl.Buffered(1))],
        out_specs=[pl.BlockSpec((tm, n), lambda i: (i, 0)) for n in P_SPLITS],
        compiler_params=_cparams(("parallel",), 56),
        name="in_proj",
    )(h, g, w)


def _mla_prep_body(cq_ref, ckv_ref, tab_ref, qn_ref, kvn_ref, wq_ref, wkv_ref, q_ref, k_ref, v_ref):
    nq = A_HEADS * NOPE_DIM
    nr = A_HEADS * ROPE_DIM
    scale = QK_DIM ** -0.5
    tab = tab_ref[0]
    cos2 = tab[:, 0:LANES]
    sin2 = tab[:, LANES:2 * LANES]
    cs = tab[:, 2 * LANES:3 * LANES]

    cqn = _rms(cq_ref[0], qn_ref[...]).astype(BF16)
    qa = _dot(cqn, wq_ref[...])
    cos_h = jnp.concatenate([cos2] * (nr // LANES), axis=1)
    sin_h = jnp.concatenate([sin2] * (nr // LANES), axis=1)
    q_rot = qa[:, nq:nq + nr] * cos_h + qa[:, nq + nr:nq + 2 * nr] * sin_h

    ckv = ckv_ref[0]
    ckvn = _rms(ckv[:, :KV_LORA], kvn_ref[...]).astype(BF16)
    kv = _dot(ckvn, wkv_ref[...])
    t = ckv[:, KV_LORA:KV_LORA + LANES] * cs
    k_rot = (t + pltpu.roll(t, ROPE_DIM, axis=1))[:, :ROPE_DIM].astype(BF16)

    for h in range(A_HEADS):
        q_ref[0, h, :, 0:NOPE_DIM] = (qa[:, h * NOPE_DIM:(h + 1) * NOPE_DIM] * scale).astype(BF16)
        q_ref[0, h, :, NOPE_DIM:QK_DIM] = (q_rot[:, h * ROPE_DIM:(h + 1) * ROPE_DIM] * scale).astype(BF16)
        k_ref[0, h, :, 0:NOPE_DIM] = kv[:, h * NOPE_DIM:(h + 1) * NOPE_DIM].astype(BF16)
        k_ref[0, h, :, NOPE_DIM:QK_DIM] = k_rot
        v_ref[0, h] = kv[:, nq + h * V_DIM:nq + (h + 1) * V_DIM].astype(BF16)


def _mla_prep(cq, ckv, tab, qn, kvn, wq, wkv):
    b, s, _ = cq.shape
    ts = SEQ_TILE
    tok = lambda n: pl.BlockSpec((1, ts, n), lambda bi, i: (bi, i, 0))
    full = lambda a: pl.BlockSpec(a.shape, lambda bi, i: (0, 0))
    head = lambda n: pl.BlockSpec((1, A_HEADS, ts, n), lambda bi, i: (bi, 0, i, 0))
    return pl.pallas_call(
        _mla_prep_body,
        out_shape=[jax.ShapeDtypeStruct((b, A_HEADS, s, QK_DIM), BF16),
                   jax.ShapeDtypeStruct((b, A_HEADS, s, QK_DIM), BF16),
                   jax.ShapeDtypeStruct((b, A_HEADS, s, V_DIM), BF16)],
        grid=(b, s // ts),
        in_specs=[tok(Q_LORA), tok(CKV_W), tok(ROPE_TW), full(qn), full(kvn), full(wq), full(wkv)],
        out_specs=[head(QK_DIM), head(QK_DIM), head(V_DIM)],
        compiler_params=_cparams(("parallel", "parallel"), 32),
        name="mla_prep",
    )(cq, ckv, tab, qn, kvn, wq, wkv)


def _attn_body(q_ref, k_ref, v_ref, g_ref, o_ref):
    i = pl.program_id(1)
    nh = q_ref.shape[1]
    tq = q_ref.shape[2]

    def block(j, carry, masked):
        start = pl.multiple_of(j * tq, tq)
        out = []
        for h in range(nh):
            m, l, acc = carry[h]
            s = _dot_nt(q_ref[0, h], k_ref[0, h, pl.ds(start, tq), :])
            if masked:
                row = lax.broadcasted_iota(jnp.int32, s.shape, 0)
                col = lax.broadcasted_iota(jnp.int32, s.shape, 1)
                s = jnp.where(col <= row, s, -1e30)
            m_new = jnp.maximum(m, jnp.max(s, axis=-1, keepdims=True))
            p = jnp.exp(s - m_new)
            alpha = jnp.exp(m - m_new)
            l = alpha * l + jnp.sum(p, axis=-1, keepdims=True)
            acc = alpha * acc + _dot(p.astype(BF16), v_ref[0, h, pl.ds(start, tq), :])
            out.append((m_new, l, acc))
        return tuple(out)

    init = tuple((jnp.full((tq, 1), -1e30, F32), jnp.zeros((tq, 1), F32), jnp.zeros((tq, V_DIM), F32))
                 for _ in range(nh))
    carry = lax.fori_loop(0, i, lambda j, c: block(j, c, False), init)
    carry = block(i, carry, True)
    for h in range(nh):
        _, l, acc = carry[h]
        hs = slice(h * V_DIM, (h + 1) * V_DIM)
        o_ref[0, :, hs] = _rms(acc / l, g_ref[:, hs]).astype(o_ref.dtype)


def _attention(q, k, v, g):
    b, nh, s, _ = q.shape
    tq = ATTN_TQ
    return pl.pallas_call(
        _attn_body,
        out_shape=jax.ShapeDtypeStruct((b, s, nh * V_DIM), BF16),
        grid=(b, s // tq),
        in_specs=[pl.BlockSpec((1, nh, tq, QK_DIM), lambda bi, i: (bi, 0, i, 0)),
                  pl.BlockSpec((1, nh, s, QK_DIM), lambda bi, i: (bi, 0, 0, 0)),
                  pl.BlockSpec((1, nh, s, V_DIM), lambda bi, i: (bi, 0, 0, 0)),
                  pl.BlockSpec((1, nh * V_DIM), lambda bi, i: (0, 0))],
        out_specs=pl.BlockSpec((1, tq, nh * V_DIM), lambda bi, i: (bi, i, 0)),
        compiler_params=_cparams(("parallel", "arbitrary"), 48),
        name="mla_attention",
    )(q, k, v, g)


def _rwkv_prep_body(pbm_ref, pbl_ref, pbm_prev_ref, pbl_prev_ref, mum_ref, mul_ref, w0_ref, a0_ref,
                    dec_hi_ref, dec_lo_ref, icl_hi_ref, icl_lo_ref, gat_hi_ref, gat_lo_ref,
                    kk_ref, ka_ref, rk_ref,
                    at_ref, rt_ref, bt_ref, kt_ref, bh_ref, kh_ref, v_ref, g_ref, bonus_ref, ptot_ref):
    i = pl.program_id(1)
    ts = pbm_ref.shape[1]
    w = B_WIDTH
    first = i == 0

    xm = pbm_ref[0]
    xl = pbl_ref[0]
    pm = jnp.where(first, 0.0, pbm_prev_ref[0])
    pv = jnp.where(first, 0.0, pbl_prev_ref[0])
    xs = xm + (_shift_rows(pm, xm, 1) - xm) * mum_ref[...]
    ls = xl + (_shift_rows(pv, xl, 1) - xl) * mul_ref[...]
    r = xs[:, 0:w]
    k = xs[:, w:2 * w]
    v = xs[:, 2 * w:3 * w]

    wl = w0_ref[...] + _dot3(jnp.tanh(ls), dec_hi_ref[...], dec_lo_ref[...])
    z = -wl
    softplus = jnp.maximum(z, 0.0) + jnp.log1p(jnp.exp(-jnp.abs(z)))
    lw = -jnp.exp(-softplus - 0.5)
    a_ic = jax.nn.sigmoid(a0_ref[...] + _dot3(ls, icl_hi_ref[...], icl_lo_ref[...]))
    gate = _dot3(jax.nn.sigmoid(ls), gat_hi_ref[...], gat_lo_ref[...])

    seg = _group_ones(w, B_HEAD)
    kk = k * kk_ref[...]
    kkn = kk / jnp.maximum(jnp.sqrt(_dot_exact_rhs(kk * kk, seg)), 1e-12)
    kp = k * (1.0 + (a_ic - 1.0) * ka_ref[...])
    bonus = _dot_exact_rhs(r * kp * rk_ref[...], seg) * v

    row = lax.broadcasted_iota(jnp.int32, (ts, ts), 0)
    col = lax.broadcasted_iota(jnp.int32, (ts, ts), 1)
    same = (row >> CHUNK_SHIFT) == (col >> CHUNK_SHIFT)
    lw_hi, lw_lo = _split(lw)
    tri = jnp.concatenate([(same & (col <= row)).astype(BF16), (same & (col > row)).astype(BF16)], axis=0)
    cums = _dot(tri, lw_hi) + _dot(tri, lw_lo)
    cum = cums[:ts]
    cumr = cums[ts:]

    e_neg = jnp.exp(-cum)
    e_rem = jnp.exp(cumr)
    b_vec = kkn * a_ic
    at_ref[0] = (-kkn * jnp.exp(cum - lw)).astype(BF16)
    rt_ref[0] = (r * jnp.exp(cum)).astype(BF16)
    bt_ref[0] = (b_vec * e_neg).astype(BF16)
    kt_ref[0] = (kp * e_neg).astype(BF16)
    bh_ref[0] = (b_vec * e_rem).astype(BF16)
    kh_ref[0] = (kp * e_rem).astype(BF16)
    v_ref[0] = v.astype(BF16)
    g_ref[0] = gate
    bonus_ref[0] = bonus
    tot = jnp.concatenate([cum[(c + 1) * CHUNK - 1:(c + 1) * CHUNK] for c in range(ts // CHUNK)], axis=0)
    ptot_ref[0, 0] = jnp.exp(tot)


def _rwkv_prep(pbm, pbl, params):
    b, s, _ = pbm.shape
    ts = SEQ_TILE
    nc = ts // CHUNK
    w = B_WIDTH
    tok = lambda n: pl.BlockSpec((1, ts, n), lambda bi, i: (bi, i, 0))
    prev = lambda n: pl.BlockSpec(
        (1, SUBLANES, n), lambda bi, i: (bi, jnp.maximum(i * (ts // SUBLANES) - 1, 0), 0))
    full = lambda a: pl.BlockSpec(a.shape, lambda bi, i: (0, 0))
    return pl.pallas_call(
        _rwkv_prep_body,
        out_shape=[jax.ShapeDtypeStruct((b, s, w), BF16)] * 7
        + [jax.ShapeDtypeStruct((b, s, w), F32)] * 2
        + [jax.ShapeDtypeStruct((b, s // ts, nc, w), F32)],
        grid=(b, s // ts),
        in_specs=[tok(PBM_W), tok(PBL_W), prev(PBM_W), prev(PBL_W)] + [full(p) for p in params],
        out_specs=[tok(w)] * 9 + [pl.BlockSpec((1, 1, nc, w), lambda bi, i: (bi, i, 0, 0))],
        compiler_params=_cparams(("parallel", "parallel"), 40),
        name="rwkv_prep",
    )(pbm, pbl, pbm, pbl, *params)


def _rwkv_scan_body(at_ref, rt_ref, bt_ref, kt_ref, bh_ref, kh_ref, v_ref, g_ref, bonus_ref, ptot_ref,
                    gain_ref, bias_ref, o_ref, state_ref):
    n = B_HEAD
    c_len = CHUNK
    ts = at_ref.shape[1]

    @pl.when(pl.program_id(1) == 0)
    def _():
        state_ref[...] = jnp.zeros_like(state_ref)

    row = lax.broadcasted_iota(jnp.int32, (c_len, 2 * c_len), 0)
    col = lax.broadcasted_iota(jnp.int32, (c_len, 2 * c_len), 1) & (c_len - 1)
    strict2 = col < row
    incl2 = col <= row
    eye = (lax.broadcasted_iota(jnp.int32, (n, n), 0) == lax.broadcasted_iota(jnp.int32, (n, n), 1))
    eye_f = eye.astype(F32)
    zero_blk = jnp.zeros((c_len, n), BF16)
    sq_row = lax.broadcasted_iota(jnp.int32, (c_len, c_len), 0)
    sq_col = lax.broadcasted_iota(jnp.int32, (c_len, c_len), 1)
    same_block = lambda bits: (sq_row >> bits) == (sq_col >> bits)
    diag_blocks = same_block(INV_BASE_BITS)
    merge_masks = [same_block(bits + 1) & ~same_block(bits) for bits in range(INV_BASE_BITS, CHUNK_SHIFT)]

    n_chunks = ts // c_len
    heads = range(at_ref.shape[2] // n)
    units = [(c, h) for c in range(n_chunks) for h in heads]
    rows = lambda c: slice(c * c_len, (c + 1) * c_len)
    lanes = lambda h: slice(h * n, (h + 1) * n)
    load = lambda ref: {(c, h): ref[0, rows(c), lanes(h)] for c, h in units}
    at, rt, bt, kt, bh, kh, vv = (load(r) for r in (at_ref, rt_ref, bt_ref, kt_ref, bh_ref, kh_ref, v_ref))
    state = [state_ref[h] for h in heads]

    g1 = {u: _dot_nt(jnp.concatenate([at[u], rt[u]], axis=0), jnp.concatenate([bt[u], kt[u]], axis=0))
          for u in units}
    a_low = {u: jnp.where(strict2, g1[u][:c_len], 0.0) for u in units}
    r_low = {u: jnp.where(incl2, g1[u][c_len:], 0.0).astype(BF16) for u in units}

    l_mat = {u: a_low[u][:, :c_len] for u in units}
    p = {u: jnp.where(diag_blocks, l_mat[u], 0.0) for u in units}
    t_inv = {u: eye_f + p[u] for u in units}
    p = {u: p[u].astype(BF16) for u in units}
    for _ in range(INV_BASE_BITS - 1):
        p = {u: _dot(p[u], p[u]).astype(BF16) for u in units}
        t_inv = {u: t_inv[u] + _dot(t_inv[u].astype(BF16), p[u]) for u in units}
    for off_mask in merge_masks:
        t_b = {u: t_inv[u].astype(BF16) for u in units}
        t_l = {u: _dot(t_b[u], jnp.where(off_mask, l_mat[u], 0.0).astype(BF16)).astype(BF16) for u in units}
        t_inv = {u: t_inv[u] + _dot(t_l[u], t_b[u]) for u in units}

    akv = {u: _dot(a_low[u][:, c_len:].astype(BF16), vv[u]).astype(BF16) for u in units}
    x = {u: _dot(t_inv[u].astype(BF16), jnp.concatenate([at[u], akv[u]], axis=1)).astype(BF16) for u in units}
    rhs = {u: jnp.concatenate([x[u], jnp.concatenate([zero_blk, vv[u]], axis=1)], axis=0) for u in units}
    qy = {u: _dot(r_low[u], rhs[u]) for u in units}
    wmat = {u: _dot_tn(rhs[u], jnp.concatenate([bh[u], kh[u]], axis=0)) for u in units}
    mc = {(c, h): (jnp.where(eye, ptot_ref[0, 0, c:c + 1, lanes(h)], 0.0) + wmat[(c, h)][:n]).astype(BF16)
          for c, h in units}
    rp = {u: (rt[u].astype(F32) + qy[u][:, :n]).astype(BF16) for u in units}

    y = {}
    for c, h in units:
        s0 = state[h].astype(BF16)
        y[(c, h)] = _dot_nt(rp[(c, h)], s0) + qy[(c, h)][:, n:]
        state[h] = _dot(s0, mc[(c, h)]) + wmat[(c, h)][n:]

    for c in range(n_chunks):
        outs = []
        for h in heads:
            yv = y[(c, h)]
            mu = jnp.mean(yv, axis=-1, keepdims=True)
            yc = yv - mu
            var = jnp.mean(yc * yc, axis=-1, keepdims=True)
            yn = yc * lax.rsqrt(var + HEAD_NORM_EPS) * gain_ref[:, lanes(h)] + bias_ref[:, lanes(h)]
            outs.append((yn + bonus_ref[0, rows(c), lanes(h)]) * g_ref[0, rows(c), lanes(h)])
        o_ref[0, rows(c), :] = jnp.concatenate(outs, axis=1).astype(o_ref.dtype)
    for h in heads:
        state_ref[h] = state[h]


def _rwkv_scan(arrs, gain, bias):
    at = arrs[0]
    b, s, w = at.shape
    ts = SEQ_TILE
    nc = ts // CHUNK
    tok = pl.BlockSpec((1, ts, w), lambda bi, i: (bi, i, 0))
    vec = pl.BlockSpec((1, w), lambda bi, i: (0, 0))
    return pl.pallas_call(
        _rwkv_scan_body,
        out_shape=jax.ShapeDtypeStruct((b, s, w), BF16),
        grid=(b, s // ts),
        in_specs=[tok] * 9 + [pl.BlockSpec((1, 1, nc, w), lambda bi, i: (bi, i, 0, 0)), vec, vec],
        out_specs=tok,
        scratch_shapes=[pltpu.VMEM((w // B_HEAD, B_HEAD, B_HEAD), F32)],
        compiler_params=_cparams(("parallel", "arbitrary"), 32),
        name="rwkv_scan",
    )(*arrs, gain, bias)


def _conv_body(pc_ref, prev_ref, w_ref, g_ref, o_ref):
    w = C_WIDTH
    x = pc_ref[0]
    xp = jnp.where(pl.program_id(1) == 0, 0.0, prev_ref[0])
    u = x[:, w:2 * w] * x[:, 2 * w:3 * w]
    up = xp[:, w:2 * w] * xp[:, 2 * w:3 * w]
    cw = w_ref[...]
    y = cw[0:1] * _shift_rows(up, u, 2) + cw[1:2] * _shift_rows(up, u, 1) + cw[2:3] * u
    z = x[:, 0:w] * y
    ms = _dot_exact_rhs(z * z, _group_ones(w, w // C_GROUPS)) * (C_GROUPS / w)
    o_ref[0] = (z * lax.rsqrt(ms + RMS_EPS) * g_ref[...]).astype(o_ref.dtype)


def _short_conv(pc, conv_w, gain):
    b, s, _ = pc.shape
    ts = SEQ_TILE
    return pl.pallas_call(
        _conv_body,
        out_shape=jax.ShapeDtypeStruct((b, s, C_WIDTH), BF16),
        grid=(b, s // ts),
        in_specs=[pl.BlockSpec((1, ts, PC_W), lambda bi, i: (bi, i, 0)),
                  pl.BlockSpec((1, SUBLANES, PC_W),
                               lambda bi, i: (bi, jnp.maximum(i * (ts // SUBLANES) - 1, 0), 0)),
                  pl.BlockSpec(conv_w.shape, lambda bi, i: (0, 0)),
                  pl.BlockSpec(gain.shape, lambda bi, i: (0, 0))],
        out_specs=pl.BlockSpec((1, ts, C_WIDTH), lambda bi, i: (bi, i, 0)),
        compiler_params=_cparams(("parallel", "parallel"), 32),
        name="short_conv",
    )(pc, pc, conv_w, gain)


def _outproj_body(h_ref, ya_ref, yb_ref, yc_ref, w_ref, o_ref):
    acc = h_ref[...] + _dot(ya_ref[...], w_ref[0:A_WIDTH, :])
    acc = acc + _dot(yb_ref[...], w_ref[A_WIDTH:A_WIDTH + B_WIDTH, :])
    o_ref[...] = acc + _dot(yc_ref[...], w_ref[A_WIDTH + B_WIDTH:, :])


def _out_proj(h, ya, yb, yc, w):
    t, d = h.shape
    tm = PROJ_TM
    row = lambda n: pl.BlockSpec((tm, n), lambda i: (i, 0))
    return pl.pallas_call(
        _outproj_body,
        out_shape=jax.ShapeDtypeStruct((t, d), F32),
        grid=(t // tm,),
        in_specs=[row(d), row(A_WIDTH), row(B_WIDTH), row(C_WIDTH),
                  pl.BlockSpec(w.shape, lambda i: (0, 0))],
        out_specs=row(d),
        compiler_params=_cparams(("parallel",), 48),
        name="out_proj",
    )(h, ya, yb, yc, w)


def _swap_halves(w):
    half = ROPE_DIM // 2
    return jnp.concatenate([w[..., half:], w[..., :half]], axis=-1)


def _arrange_w_in(w_in):
    d = w_in.shape[0]
    o = Q_LORA + KV_LORA
    k_rope = w_in[:, o:A_IN]
    b0 = A_IN
    lora = w_in[:, b0 + PBM_W:b0 + B_IN]
    pad = jnp.zeros((d, PBL_W - lora.shape[1]), w_in.dtype)
    return jnp.concatenate(
        [w_in[:, :Q_LORA], w_in[:, Q_LORA:o], k_rope, _swap_halves(k_rope),
         w_in[:, b0:b0 + PBM_W], lora, pad, w_in[:, b0 + B_IN:]], axis=1).astype(BF16)


def _arrange_w_uq(w_uq):
    w = w_uq.reshape(Q_LORA, A_HEADS, QK_DIM)
    nope = w[:, :, :NOPE_DIM].reshape(Q_LORA, -1)
    rope = w[:, :, NOPE_DIM:]
    return jnp.concatenate(
        [nope, rope.reshape(Q_LORA, -1), _swap_halves(rope).reshape(Q_LORA, -1)], axis=1).astype(BF16)


def _arrange_w_ukv(w_ukv):
    w = w_ukv.reshape(KV_LORA, A_HEADS, NOPE_DIM + V_DIM)
    return jnp.concatenate(
        [w[:, :, :NOPE_DIM].reshape(KV_LORA, -1), w[:, :, NOPE_DIM:].reshape(KV_LORA, -1)], axis=1).astype(BF16)


def _pad_rows(w, start):
    full = jnp.zeros((PBL_W, w.shape[1]), F32).at[start:start + w.shape[0]].set(w)
    hi = full.astype(BF16)
    return hi, (full - hi.astype(F32)).astype(BF16)


def _row(v):
    return v.reshape(1, -1)


def kernel(x, positions, norm_ffn1, ffn1_gate, ffn1_up, ffn1_down, norm_mix, w_in, q_norm, kv_norm, w_uq, w_ukv, attn_out_norm, shift_mu, decay_w0, decay_up, iclr_a0, iclr_up, gate_up, k_k, k_a, r_k, lnx_gain, lnx_bias, conv_w, conv_out_norm, w_out, norm_ffn2, ffn2_gate, ffn2_up, ffn2_down, norm_final):
    b, s, d = x.shape
    t = b * s
    seq = lambda a: a.reshape(b, s, a.shape[-1])

    inv_freq = 1.0 / (ROPE_THETA ** (jnp.arange(0, ROPE_DIM, 2, dtype=F32) / ROPE_DIM))
    inv_lane = jnp.tile(inv_freq, LANES // inv_freq.shape[0]).reshape(1, LANES)
    tab = seq(_rope_table(positions.reshape(t, 1), inv_lane))

    h = x.reshape(t, d)
    g_final = _row(norm_final)
    for l in range(DEPTH):
        h = _ffn(h, _row(norm_ffn1[l]), ffn1_gate[l].astype(BF16), ffn1_up[l].astype(BF16),
                 ffn1_down[l].astype(BF16), g_final, False)

        cq, ckv, pbm, pbl, pc = _in_proj(h, _row(norm_mix[l]), _arrange_w_in(w_in[l]))

        q, k, v = _mla_prep(seq(cq), seq(ckv), tab, _row(q_norm[l]), _row(kv_norm[l]),
                            _arrange_w_uq(w_uq[l]), _arrange_w_ukv(w_ukv[l]))
        ya = _attention(q, k, v, _row(attn_out_norm[l]))

        mu = shift_mu[l]
        mu_l = jnp.zeros((PBL_W,), F32).at[:B_IN - PBM_W].set(mu[PBM_W:])
        rw_params = (_row(mu[:PBM_W]), _row(mu_l), _row(decay_w0[l]), _row(iclr_a0[l]),
                     *_pad_rows(decay_up[l], 0), *_pad_rows(iclr_up[l], DECAY_LORA),
                     *_pad_rows(gate_up[l], DECAY_LORA + ICLR_LORA),
                     _row(k_k[l]), _row(k_a[l]), _row(r_k[l]))
        rw = _rwkv_prep(seq(pbm), seq(pbl), rw_params)
        yb = _rwkv_scan(rw, _row(lnx_gain[l]), _row(lnx_bias[l]))

        yc = _short_conv(seq(pc), conv_w[l], _row(conv_out_norm[l]))

        h = _out_proj(h, ya.reshape(t, A_WIDTH), yb.reshape(t, B_WIDTH), yc.reshape(t, C_WIDTH),
                      w_out[l].astype(BF16))

        h = _ffn(h, _row(norm_ffn2[l]), ffn2_gate[l].astype(BF16), ffn2_up[l].astype(BF16),
                 ffn2_down[l].astype(BF16), g_final, l == DEPTH - 1)
    return h.reshape(b, s, d)
```

```python
import functools

import jax
import jax.numpy as jnp
from jax import lax
from jax.experimental import pallas as pl
from jax.experimental.pallas import tpu as pltpu

F32 = jnp.float32
BF16 = jnp.bfloat16

D_MODEL = 2048
DEPTH = 4
A_WIDTH = 1024
B_WIDTH = 512
C_WIDTH = 512
V_DIM = 128
NOPE_DIM = 128
ROPE_DIM = 64
QK_DIM = NOPE_DIM + ROPE_DIM
A_HEADS = 8
Q_LORA = 512
KV_LORA = 256
ROPE_THETA = 10000.0
B_HEAD = 64
B_HEADS = 8
DECAY_LORA = 32
ICLR_LORA = 32
GATE_LORA = 96
HEAD_NORM_EPS = 64e-5
C_GROUPS = 8
CONV_K = 3
D_FF = 5632
RMS_EPS = 1e-6
LOG2_E = 1.4426950408889634
A_IN = Q_LORA + KV_LORA + ROPE_DIM
B_IN = 3 * B_WIDTH + DECAY_LORA + ICLR_LORA + GATE_LORA

CKV_W = KV_LORA + 2 * ROPE_DIM
PBM_W = 3 * B_WIDTH
PBL_W = 256
PC_W = 3 * C_WIDTH
P_SPLITS = (Q_LORA, CKV_W, PBM_W, PBL_W, PC_W)
P_TOTAL = sum(P_SPLITS)

V7X_VMEM_BYTES = 64 * 1024 * 1024
SUBLANES = 8
LANES = 128
FFN_TM = 512
FFN_TF = 512
PROJ_TM = 512
SEQ_TILE = 256
ATTN_TQ = 256
CHUNK = 64
CHUNK_SHIFT = CHUNK.bit_length() - 1
INV_BASE_BITS = 3
ROPE_TW = 3 * LANES


def _cparams(sem, vmem_mb):
    return pltpu.CompilerParams(dimension_semantics=sem, vmem_limit_bytes=vmem_mb * 1024 * 1024)


def _rms(x, g, eps=RMS_EPS):
    return x * lax.rsqrt(jnp.mean(x * x, axis=-1, keepdims=True) + eps) * g


def _dot(a, b):
    return jnp.dot(a, b, preferred_element_type=F32)


def _dot_nt(a, b):
    return lax.dot_general(a, b, (((1,), (1,)), ((), ())), preferred_element_type=F32)


def _dot_tn(a, b):
    return lax.dot_general(a, b, (((0,), (0,)), ((), ())), preferred_element_type=F32)


def _split(x):
    hi = x.astype(BF16)
    lo = (x - hi.astype(F32)).astype(BF16)
    return hi, lo


def _dot_exact_rhs(x, m):
    hi, lo = _split(x)
    return _dot(hi, m) + _dot(lo, m)


def _dot3(x, w_hi, w_lo):
    hi, lo = _split(x)
    return _dot(hi, w_hi) + (_dot(lo, w_hi) + _dot(hi, w_lo))


def _shift_rows(prev8, x, k):
    cat = jnp.concatenate([prev8, x], axis=0)
    return pltpu.roll(cat, k, axis=0)[SUBLANES:]


def _group_ones(n, group):
    shift = group.bit_length() - 1
    r = lax.broadcasted_iota(jnp.int32, (n, n), 0) >> shift
    c = lax.broadcasted_iota(jnp.int32, (n, n), 1) >> shift
    return (r == c).astype(BF16)


def _rope_body(pos_ref, inv_ref, o_ref):
    ang = pos_ref[...].astype(F32) * inv_ref[...]
    c = jnp.cos(ang)
    s = jnp.sin(ang)
    lane = lax.broadcasted_iota(jnp.int32, ang.shape, 1)
    ssg = jnp.where((lane & (ROPE_DIM - 1)) < ROPE_DIM // 2, -s, s)
    o_ref[:, 0:LANES] = c
    o_ref[:, LANES:2 * LANES] = ssg
    o_ref[:, 2 * LANES:3 * LANES] = jnp.where(lane < ROPE_DIM, c, ssg)


def _rope_table(pos_col, inv_lane):
    t = pos_col.shape[0]
    tm = 1024
    return pl.pallas_call(
        _rope_body,
        out_shape=jax.ShapeDtypeStruct((t, ROPE_TW), F32),
        grid=(t // tm,),
        in_specs=[pl.BlockSpec((tm, 1), lambda i: (i, 0)),
                  pl.BlockSpec((1, LANES), lambda i: (0, 0))],
        out_specs=pl.BlockSpec((tm, ROPE_TW), lambda i: (i, 0)),
        compiler_params=_cparams(("parallel",), 16),
        name="rope_table",
    )(pos_col, inv_lane)


def _ffn_body(x_ref, g_ref, wg_ref, wu_ref, wd_ref, gf_ref, o_ref, xn_ref, *, final):
    j = pl.program_id(1)

    @pl.when(j == 0)
    def _():
        x = x_ref[...]
        xn_ref[...] = _rms(x, g_ref[...]).astype(BF16)
        o_ref[...] = x

    xn = xn_ref[...]
    gate = _dot(xn, wg_ref[...])
    up = _dot(xn, wu_ref[...])
    act = (gate * jax.nn.sigmoid(gate)) * (up * 0.5)
    o_ref[...] += _dot(act.astype(BF16), wd_ref[...])

    if final:
        @pl.when(j == pl.num_programs(1) - 1)
        def _():
            o_ref[...] = _rms(o_ref[...], gf_ref[...])


def _ffn(h, g, wg, wu, wd, g_final, final):
    t, d = h.shape
    f = wg.shape[1]
    tm, tf = FFN_TM, FFN_TF
    return pl.pallas_call(
        functools.partial(_ffn_body, final=final),
        out_shape=jax.ShapeDtypeStruct((t, d), F32),
        grid=(t // tm, f // tf),
        in_specs=[pl.BlockSpec((tm, d), lambda i, j: (i, 0)),
                  pl.BlockSpec((1, d), lambda i, j: (0, 0)),
                  pl.BlockSpec((d, tf), lambda i, j: (0, j)),
                  pl.BlockSpec((d, tf), lambda i, j: (0, j)),
                  pl.BlockSpec((tf, d), lambda i, j: (j, 0)),
                  pl.BlockSpec((1, d), lambda i, j: (0, 0))],
        out_specs=pl.BlockSpec((tm, d), lambda i, j: (i, 0)),
        scratch_shapes=[pltpu.VMEM((tm, d), BF16)],
        compiler_params=_cparams(("parallel", "arbitrary"), 48),
        name="ffn",
    )(h, g, wg, wu, wd, g_final)


def _inproj_body(x_ref, g_ref, w_ref, cq_ref, ckv_ref, pbm_ref, pbl_ref, pc_ref):
    xn = _rms(x_ref[...], g_ref[...]).astype(BF16)
    off = 0
    for ref in (cq_ref, ckv_ref, pbm_ref, pbl_ref, pc_ref):
        n = ref.shape[-1]
        ref[...] = _dot(xn, w_ref[:, off:off + n])
        off += n


def _in_proj(h, g, w):
    t, d = h.shape
    tm = PROJ_TM
    return pl.pallas_call(
        _inproj_body,
        out_shape=[jax.ShapeDtypeStruct((t, n), F32) for n in P_SPLITS],
        grid=(t // tm,),
        in_specs=[pl.BlockSpec((tm, d), lambda i: (i, 0)),
                  pl.BlockSpec((1, d), lambda i: (0, 0)),
                  pl.BlockSpec((d, P_TOTAL), lambda i: (0, 0), pipeline_mode=pl.Buffered(1))],
        out_specs=[pl.BlockSpec((tm, n), lambda i: (i, 0)) for n in P_SPLITS],
        compiler_params=_cparams(("parallel",), 56),
        name="in_proj",
    )(h, g, w)


def _mla_prep_body(cq_ref, ckv_ref, tab_ref, qn_ref, kvn_ref, wq_ref, wkv_ref, q_ref, k_ref, v_ref):
    nq = A_HEADS * NOPE_DIM
    nr = A_HEADS * ROPE_DIM
    scale = QK_DIM ** -0.5 * LOG2_E
    tab = tab_ref[0]
    cos2 = tab[:, 0:LANES]
    sin2 = tab[:, LANES:2 * LANES]
    cs = tab[:, 2 * LANES:3 * LANES]

    cqn = _rms(cq_ref[0], qn_ref[...]).astype(BF16)
    qa = _dot(cqn, wq_ref[...])
    cos_h = jnp.concatenate([cos2] * (nr // LANES), axis=1)
    sin_h = jnp.concatenate([sin2] * (nr // LANES), axis=1)
    q_rot = qa[:, nq:nq + nr] * cos_h + qa[:, nq + nr:nq + 2 * nr] * sin_h

    ckv = ckv_ref[0]
    ckvn = _rms(ckv[:, :KV_LORA], kvn_ref[...]).astype(BF16)
    kv = _dot(ckvn, wkv_ref[...])
    t = ckv[:, KV_LORA:KV_LORA + LANES] * cs
    k_rot = (t + pltpu.roll(t, ROPE_DIM, axis=1))[:, :ROPE_DIM].astype(BF16)

    for h in range(A_HEADS):
        q_ref[0, h, :, 0:NOPE_DIM] = (qa[:, h * NOPE_DIM:(h + 1) * NOPE_DIM] * scale).astype(BF16)
        q_ref[0, h, :, NOPE_DIM:QK_DIM] = (q_rot[:, h * ROPE_DIM:(h + 1) * ROPE_DIM] * scale).astype(BF16)
        k_ref[0, h, :, 0:NOPE_DIM] = kv[:, h * NOPE_DIM:(h + 1) * NOPE_DIM].astype(BF16)
        k_ref[0, h, :, NOPE_DIM:QK_DIM] = k_rot
        v_ref[0, h] = kv[:, nq + h * V_DIM:nq + (h + 1) * V_DIM].astype(BF16)


def _mla_prep(cq, ckv, tab, qn, kvn, wq, wkv):
    b, s, _ = cq.shape
    ts = SEQ_TILE
    tok = lambda n: pl.BlockSpec((1, ts, n), lambda bi, i: (bi, i, 0))
    full = lambda a: pl.BlockSpec(a.shape, lambda bi, i: (0, 0))
    head = lambda n: pl.BlockSpec((1, A_HEADS, ts, n), lambda bi, i: (bi, 0, i, 0))
    return pl.pallas_call(
        _mla_prep_body,
        out_shape=[jax.ShapeDtypeStruct((b, A_HEADS, s, QK_DIM), BF16),
                   jax.ShapeDtypeStruct((b, A_HEADS, s, QK_DIM), BF16),
                   jax.ShapeDtypeStruct((b, A_HEADS, s, V_DIM), BF16)],
        grid=(b, s // ts),
        in_specs=[tok(Q_LORA), tok(CKV_W), tok(ROPE_TW), full(qn), full(kvn), full(wq), full(wkv)],
        out_specs=[head(QK_DIM), head(QK_DIM), head(V_DIM)],
        compiler_params=_cparams(("parallel", "parallel"), 32),
        name="mla_prep",
    )(cq, ckv, tab, qn, kvn, wq, wkv)


def _attn_body(q_ref, k_ref, v_ref, g_ref, o_ref, s_ref, mx_ref, m_ref, acc_ref):
    i = pl.program_id(1)
    nh = q_ref.shape[1]
    tq = q_ref.shape[2]
    heads = range(nh)
    ones = jnp.ones((tq, V_DIM), BF16)
    keys = lambda j: pl.ds(pl.multiple_of(j * tq, tq), tq)

    row = lax.broadcasted_iota(jnp.int32, (tq, tq), 0)
    col = lax.broadcasted_iota(jnp.int32, (tq, tq), 1)
    for h in heads:
        s = jnp.where(col <= row, _dot_nt(q_ref[0, h], k_ref[0, h, keys(i), :]), -1e30)
        s_ref[h, i] = s
        mx_ref[h] = s

    def scores(j, _):
        for h in heads:
            s = _dot_nt(q_ref[0, h], k_ref[0, h, keys(j), :])
            s_ref[h, j] = s
            mx_ref[h] = jnp.maximum(mx_ref[h], s)
        return 0

    lax.fori_loop(0, i, scores, 0)
    for h in heads:
        m_ref[h] = jnp.broadcast_to(jnp.max(mx_ref[h], axis=-1, keepdims=True), (tq, LANES))

    def weighted(j, h):
        m = m_ref[h]
        p = jnp.exp2(s_ref[h, j] - jnp.concatenate([m] * (tq // LANES), axis=1))
        return _dot(p.astype(BF16), jnp.concatenate([v_ref[0, h, keys(j), :], ones], axis=1))

    for h in heads:
        acc_ref[h] = weighted(i, h)

    def values(j, _):
        for h in heads:
            acc_ref[h] += weighted(j, h)
        return 0

    lax.fori_loop(0, i, values, 0)
    for h in heads:
        acc = acc_ref[h]
        hs = slice(h * V_DIM, (h + 1) * V_DIM)
        o_ref[0, :, hs] = _rms(acc[:, :V_DIM] / acc[:, V_DIM:], g_ref[:, hs]).astype(o_ref.dtype)


def _attention(q, k, v, g):
    b, nh, s, _ = q.shape
    tq = ATTN_TQ
    return pl.pallas_call(
        _attn_body,
        out_shape=jax.ShapeDtypeStruct((b, s, nh * V_DIM), BF16),
        grid=(b, s // tq),
        in_specs=[pl.BlockSpec((1, nh, tq, QK_DIM), lambda bi, i: (bi, 0, i, 0)),
                  pl.BlockSpec((1, nh, s, QK_DIM), lambda bi, i: (bi, 0, 0, 0)),
                  pl.BlockSpec((1, nh, s, V_DIM), lambda bi, i: (bi, 0, 0, 0)),
                  pl.BlockSpec((1, nh * V_DIM), lambda bi, i: (0, 0))],
        out_specs=pl.BlockSpec((1, tq, nh * V_DIM), lambda bi, i: (bi, i, 0)),
        scratch_shapes=[pltpu.VMEM((nh, s // tq, tq, tq), F32),
                        pltpu.VMEM((nh, tq, tq), F32),
                        pltpu.VMEM((nh, tq, LANES), F32),
                        pltpu.VMEM((nh, tq, 2 * V_DIM), F32)],
        compiler_params=_cparams(("parallel", "arbitrary"), 56),
        name="mla_attention",
    )(q, k, v, g)


def _rwkv_prep_body(pbm_ref, pbl_ref, pbm_prev_ref, pbl_prev_ref, mum_ref, mul_ref, w0_ref, a0_ref,
                    dec_hi_ref, dec_lo_ref, icl_hi_ref, icl_lo_ref, gat_hi_ref, gat_lo_ref,
                    kk_ref, ka_ref, rk_ref,
                    at_ref, rt_ref, bt_ref, kt_ref, bh_ref, kh_ref, v_ref, g_ref, bonus_ref, ptot_ref):
    i = pl.program_id(1)
    ts = pbm_ref.shape[1]
    w = B_WIDTH
    first = i == 0

    xm = pbm_ref[0]
    xl = pbl_ref[0]
    pm = jnp.where(first, 0.0, pbm_prev_ref[0])
    pv = jnp.where(first, 0.0, pbl_prev_ref[0])
    xs = xm + (_shift_rows(pm, xm, 1) - xm) * mum_ref[...]
    ls = xl + (_shift_rows(pv, xl, 1) - xl) * mul_ref[...]
    r = xs[:, 0:w]
    k = xs[:, w:2 * w]
    v = xs[:, 2 * w:3 * w]

    wl = w0_ref[...] + _dot3(jnp.tanh(ls), dec_hi_ref[...], dec_lo_ref[...])
    z = -wl
    softplus = jnp.maximum(z, 0.0) + jnp.log1p(jnp.exp(-jnp.abs(z)))
    lw = -jnp.exp(-softplus - 0.5)
    a_ic = jax.nn.sigmoid(a0_ref[...] + _dot3(ls, icl_hi_ref[...], icl_lo_ref[...]))
    gate = _dot3(jax.nn.sigmoid(ls), gat_hi_ref[...], gat_lo_ref[...])

    seg = _group_ones(w, B_HEAD)
    kk = k * kk_ref[...]
    kkn = kk / jnp.maximum(jnp.sqrt(_dot_exact_rhs(kk * kk, seg)), 1e-12)
    kp = k * (1.0 + (a_ic - 1.0) * ka_ref[...])
    bonus = _dot_exact_rhs(r * kp * rk_ref[...], seg) * v

    row = lax.broadcasted_iota(jnp.int32, (ts, ts), 0)
    col = lax.broadcasted_iota(jnp.int32, (ts, ts), 1)
    same = (row >> CHUNK_SHIFT) == (col >> CHUNK_SHIFT)
    lw_hi, lw_lo = _split(lw)
    tri = jnp.concatenate([(same & (col <= row)).astype(BF16), (same & (col > row)).astype(BF16)], axis=0)
    cums = _dot(tri, lw_hi) + _dot(tri, lw_lo)
    cum = cums[:ts]
    cumr = cums[ts:]

    e_neg = jnp.exp(-cum)
    e_rem = jnp.exp(cumr)
    b_vec = kkn * a_ic
    at_ref[0] = (-kkn * jnp.exp(cum - lw)).astype(BF16)
    rt_ref[0] = (r * jnp.exp(cum)).astype(BF16)
    bt_ref[0] = (b_vec * e_neg).astype(BF16)
    kt_ref[0] = (kp * e_neg).astype(BF16)
    bh_ref[0] = (b_vec * e_rem).astype(BF16)
    kh_ref[0] = (kp * e_rem).astype(BF16)
    v_ref[0] = v.astype(BF16)
    g_ref[0] = gate
    bonus_ref[0] = bonus
    tot = jnp.concatenate([cum[(c + 1) * CHUNK - 1:(c + 1) * CHUNK] for c in range(ts // CHUNK)], axis=0)
    ptot_ref[0, 0] = jnp.exp(tot)


def _rwkv_prep(pbm, pbl, params):
    b, s, _ = pbm.shape
    ts = SEQ_TILE
    nc = ts // CHUNK
    w = B_WIDTH
    tok = lambda n: pl.BlockSpec((1, ts, n), lambda bi, i: (bi, i, 0))
    prev = lambda n: pl.BlockSpec(
        (1, SUBLANES, n), lambda bi, i: (bi, jnp.maximum(i * (ts // SUBLANES) - 1, 0), 0))
    full = lambda a: pl.BlockSpec(a.shape, lambda bi, i: (0, 0))
    return pl.pallas_call(
        _rwkv_prep_body,
        out_shape=[jax.ShapeDtypeStruct((b, s, w), BF16)] * 7
        + [jax.ShapeDtypeStruct((b, s, w), F32)] * 2
        + [jax.ShapeDtypeStruct((b, s // ts, nc, w), F32)],
        grid=(b, s // ts),
        in_specs=[tok(PBM_W), tok(PBL_W), prev(PBM_W), prev(PBL_W)] + [full(p) for p in params],
        out_specs=[tok(w)] * 9 + [pl.BlockSpec((1, 1, nc, w), lambda bi, i: (bi, i, 0, 0))],
        compiler_params=_cparams(("parallel", "parallel"), 40),
        name="rwkv_prep",
    )(pbm, pbl, pbm, pbl, *params)


def _rwkv_scan_body(at_ref, rt_ref, bt_ref, kt_ref, bh_ref, kh_ref, v_ref, g_ref, bonus_ref, ptot_ref,
                    gain_ref, bias_ref, o_ref, state_ref):
    n = B_HEAD
    c_len = CHUNK
    ts = at_ref.shape[1]

    @pl.when(pl.program_id(1) == 0)
    def _():
        state_ref[...] = jnp.zeros_like(state_ref)

    row = lax.broadcasted_iota(jnp.int32, (c_len, 2 * c_len), 0)
    col = lax.broadcasted_iota(jnp.int32, (c_len, 2 * c_len), 1) & (c_len - 1)
    strict2 = col < row
    incl2 = col <= row
    eye = (lax.broadcasted_iota(jnp.int32, (n, n), 0) == lax.broadcasted_iota(jnp.int32, (n, n), 1))
    eye_f = eye.astype(F32)
    zero_blk = jnp.zeros((c_len, n), BF16)
    sq_row = lax.broadcasted_iota(jnp.int32, (c_len, c_len), 0)
    sq_col = lax.broadcasted_iota(jnp.int32, (c_len, c_len), 1)
    same_block = lambda bits: (sq_row >> bits) == (sq_col >> bits)
    diag_blocks = same_block(INV_BASE_BITS)
    merge_masks = [same_block(bits + 1) & ~same_block(bits) for bits in range(INV_BASE_BITS, CHUNK_SHIFT)]

    n_chunks = ts // c_len
    heads = range(at_ref.shape[2] // n)
    units = [(c, h) for c in range(n_chunks) for h in heads]
    rows = lambda c: slice(c * c_len, (c + 1) * c_len)
    lanes = lambda h: slice(h * n, (h + 1) * n)
    load = lambda ref: {(c, h): ref[0, rows(c), lanes(h)] for c, h in units}
    at, rt, bt, kt, bh, kh, vv = (load(r) for r in (at_ref, rt_ref, bt_ref, kt_ref, bh_ref, kh_ref, v_ref))
    state = [state_ref[h] for h in heads]

    g1 = {u: _dot_nt(jnp.concatenate([at[u], rt[u]], axis=0), jnp.concatenate([bt[u], kt[u]], axis=0))
          for u in units}
    a_low = {u: jnp.where(strict2, g1[u][:c_len], 0.0) for u in units}
    r_low = {u: jnp.where(incl2, g1[u][c_len:], 0.0).astype(BF16) for u in units}

    l_mat = {u: a_low[u][:, :c_len] for u in units}
    p = {u: jnp.where(diag_blocks, l_mat[u], 0.0) for u in units}
    t_inv = {u: eye_f + p[u] for u in units}
    p = {u: p[u].astype(BF16) for u in units}
    for _ in range(INV_BASE_BITS - 1):
        p = {u: _dot(p[u], p[u]).astype(BF16) for u in units}
        t_inv = {u: t_inv[u] + _dot(t_inv[u].astype(BF16), p[u]) for u in units}
    for off_mask in merge_masks:
        t_b = {u: t_inv[u].astype(BF16) for u in units}
        t_l = {u: _dot(t_b[u], jnp.where(off_mask, l_mat[u], 0.0).astype(BF16)).astype(BF16) for u in units}
        t_inv = {u: t_inv[u] + _dot(t_l[u], t_b[u]) for u in units}

    akv = {u: _dot(a_low[u][:, c_len:].astype(BF16), vv[u]).astype(BF16) for u in units}
    x = {u: _dot(t_inv[u].astype(BF16), jnp.concatenate([at[u], akv[u]], axis=1)).astype(BF16) for u in units}
    rhs = {u: jnp.concatenate([x[u], jnp.concatenate([zero_blk, vv[u]], axis=1)], axis=0) for u in units}
    qy = {u: _dot(r_low[u], rhs[u]) for u in units}
    wmat = {u: _dot_tn(rhs[u], jnp.concatenate([bh[u], kh[u]], axis=0)) for u in units}
    mc = {(c, h): (jnp.where(eye, ptot_ref[0, 0, c:c + 1, lanes(h)], 0.0) + wmat[(c, h)][:n]).astype(BF16)
          for c, h in units}
    rp = {u: (rt[u].astype(F32) + qy[u][:, :n]).astype(BF16) for u in units}

    y = {}
    for c, h in units:
        s0 = state[h].astype(BF16)
        y[(c, h)] = _dot_nt(rp[(c, h)], s0) + qy[(c, h)][:, n:]
        state[h] = _dot(s0, mc[(c, h)]) + wmat[(c, h)][n:]

    for c in range(n_chunks):
        outs = []
        for h in heads:
            yv = y[(c, h)]
            mu = jnp.mean(yv, axis=-1, keepdims=True)
            yc = yv - mu
            var = jnp.mean(yc * yc, axis=-1, keepdims=True)
            yn = yc * lax.rsqrt(var + HEAD_NORM_EPS) * gain_ref[:, lanes(h)] + bias_ref[:, lanes(h)]
            outs.append((yn + bonus_ref[0, rows(c), lanes(h)]) * g_ref[0, rows(c), lanes(h)])
        o_ref[0, rows(c), :] = jnp.concatenate(outs, axis=1).astype(o_ref.dtype)
    for h in heads:
        state_ref[h] = state[h]


def _rwkv_scan(arrs, gain, bias):
    at = arrs[0]
    b, s, w = at.shape
    ts = SEQ_TILE
    nc = ts // CHUNK
    tok = pl.BlockSpec((1, ts, w), lambda bi, i: (bi, i, 0))
    vec = pl.BlockSpec((1, w), lambda bi, i: (0, 0))
    return pl.pallas_call(
        _rwkv_scan_body,
        out_shape=jax.ShapeDtypeStruct((b, s, w), BF16),
        grid=(b, s // ts),
        in_specs=[tok] * 9 + [pl.BlockSpec((1, 1, nc, w), lambda bi, i: (bi, i, 0, 0)), vec, vec],
        out_specs=tok,
        scratch_shapes=[pltpu.VMEM((w // B_HEAD, B_HEAD, B_HEAD), F32)],
        compiler_params=_cparams(("parallel", "arbitrary"), 32),
        name="rwkv_scan",
    )(*arrs, gain, bias)


def _conv_body(pc_ref, prev_ref, w_ref, g_ref, o_ref):
    w = C_WIDTH
    x = pc_ref[0]
    xp = jnp.where(pl.program_id(1) == 0, 0.0, prev_ref[0])
    u = x[:, w:2 * w] * x[:, 2 * w:3 * w]
    up = xp[:, w:2 * w] * xp[:, 2 * w:3 * w]
    cw = w_ref[...]
    y = cw[0:1] * _shift_rows(up, u, 2) + cw[1:2] * _shift_rows(up, u, 1) + cw[2:3] * u
    z = x[:, 0:w] * y
    ms = _dot_exact_rhs(z * z, _group_ones(w, w // C_GROUPS)) * (C_GROUPS / w)
    o_ref[0] = (z * lax.rsqrt(ms + RMS_EPS) * g_ref[...]).astype(o_ref.dtype)


def _short_conv(pc, conv_w, gain):
    b, s, _ = pc.shape
    ts = SEQ_TILE
    return pl.pallas_call(
        _conv_body,
        out_shape=jax.ShapeDtypeStruct((b, s, C_WIDTH), BF16),
        grid=(b, s // ts),
        in_specs=[pl.BlockSpec((1, ts, PC_W), lambda bi, i: (bi, i, 0)),
                  pl.BlockSpec((1, SUBLANES, PC_W),
                               lambda bi, i: (bi, jnp.maximum(i * (ts // SUBLANES) - 1, 0), 0)),
                  pl.BlockSpec(conv_w.shape, lambda bi, i: (0, 0)),
                  pl.BlockSpec(gain.shape, lambda bi, i: (0, 0))],
        out_specs=pl.BlockSpec((1, ts, C_WIDTH), lambda bi, i: (bi, i, 0)),
        compiler_params=_cparams(("parallel", "parallel"), 32),
        name="short_conv",
    )(pc, pc, conv_w, gain)


def _outproj_body(h_ref, ya_ref, yb_ref, yc_ref, w_ref, o_ref):
    acc = h_ref[...] + _dot(ya_ref[...], w_ref[0:A_WIDTH, :])
    acc = acc + _dot(yb_ref[...], w_ref[A_WIDTH:A_WIDTH + B_WIDTH, :])
    o_ref[...] = acc + _dot(yc_ref[...], w_ref[A_WIDTH + B_WIDTH:, :])


def _out_proj(h, ya, yb, yc, w):
    t, d = h.shape
    tm = PROJ_TM
    row = lambda n: pl.BlockSpec((tm, n), lambda i: (i, 0))
    return pl.pallas_call(
        _outproj_body,
        out_shape=jax.ShapeDtypeStruct((t, d), F32),
        grid=(t // tm,),
        in_specs=[row(d), row(A_WIDTH), row(B_WIDTH), row(C_WIDTH),
                  pl.BlockSpec(w.shape, lambda i: (0, 0))],
        out_specs=row(d),
        compiler_params=_cparams(("parallel",), 48),
        name="out_proj",
    )(h, ya, yb, yc, w)


def _swap_halves(w):
    half = ROPE_DIM // 2
    return jnp.concatenate([w[..., half:], w[..., :half]], axis=-1)


def _arrange_w_in(w_in):
    d = w_in.shape[0]
    o = Q_LORA + KV_LORA
    k_rope = w_in[:, o:A_IN]
    b0 = A_IN
    lora = w_in[:, b0 + PBM_W:b0 + B_IN]
    pad = jnp.zeros((d, PBL_W - lora.shape[1]), w_in.dtype)
    return jnp.concatenate(
        [w_in[:, :Q_LORA], w_in[:, Q_LORA:o], k_rope, _swap_halves(k_rope),
         w_in[:, b0:b0 + PBM_W], lora, pad, w_in[:, b0 + B_IN:]], axis=1).astype(BF16)


def _arrange_w_uq(w_uq):
    w = w_uq.reshape(Q_LORA, A_HEADS, QK_DIM)
    nope = w[:, :, :NOPE_DIM].reshape(Q_LORA, -1)
    rope = w[:, :, NOPE_DIM:]
    return jnp.concatenate(
        [nope, rope.reshape(Q_LORA, -1), _swap_halves(rope).reshape(Q_LORA, -1)], axis=1).astype(BF16)


def _arrange_w_ukv(w_ukv):
    w = w_ukv.reshape(KV_LORA, A_HEADS, NOPE_DIM + V_DIM)
    return jnp.concatenate(
        [w[:, :, :NOPE_DIM].reshape(KV_LORA, -1), w[:, :, NOPE_DIM:].reshape(KV_LORA, -1)], axis=1).astype(BF16)


def _pad_rows(w, start):
    full = jnp.zeros((PBL_W, w.shape[1]), F32).at[start:start + w.shape[0]].set(w)
    hi = full.astype(BF16)
    return hi, (full - hi.astype(F32)).astype(BF16)


def _row(v):
    return v.reshape(1, -1)


def kernel(x, positions, norm_ffn1, ffn1_gate, ffn1_up, ffn1_down, norm_mix, w_in, q_norm, kv_norm, w_uq, w_ukv, attn_out_norm, shift_mu, decay_w0, decay_up, iclr_a0, iclr_up, gate_up, k_k, k_a, r_k, lnx_gain, lnx_bias, conv_w, conv_out_norm, w_out, norm_ffn2, ffn2_gate, ffn2_up, ffn2_down, norm_final):
    b, s, d = x.shape
    t = b * s
    seq = lambda a: a.reshape(b, s, a.shape[-1])

    inv_freq = 1.0 / (ROPE_THETA ** (jnp.arange(0, ROPE_DIM, 2, dtype=F32) / ROPE_DIM))
    inv_lane = jnp.tile(inv_freq, LANES // inv_freq.shape[0]).reshape(1, LANES)
    tab = seq(_rope_table(positions.reshape(t, 1), inv_lane))

    h = x.reshape(t, d)
    g_final = _row(norm_final)
    for l in range(DEPTH):
        h = _ffn(h, _row(norm_ffn1[l]), ffn1_gate[l].astype(BF16), ffn1_up[l].astype(BF16),
                 ffn1_down[l].astype(BF16), g_final, False)

        cq, ckv, pbm, pbl, pc = _in_proj(h, _row(norm_mix[l]), _arrange_w_in(w_in[l]))

        q, k, v = _mla_prep(seq(cq), seq(ckv), tab, _row(q_norm[l]), _row(kv_norm[l]),
                            _arrange_w_uq(w_uq[l]), _arrange_w_ukv(w_ukv[l]))
        ya = _attention(q, k, v, _row(attn_out_norm[l]))

        mu = shift_mu[l]
        mu_l = jnp.zeros((PBL_W,), F32).at[:B_IN - PBM_W].set(mu[PBM_W:])
        rw_params = (_row(mu[:PBM_W]), _row(mu_l), _row(decay_w0[l]), _row(iclr_a0[l]),
                     *_pad_rows(decay_up[l], 0), *_pad_rows(iclr_up[l], DECAY_LORA),
                     *_pad_rows(gate_up[l], DECAY_LORA + ICLR_LORA),
                     _row(k_k[l]), _row(k_a[l]), _row(r_k[l]))
        rw = _rwkv_prep(seq(pbm), seq(pbl), rw_params)
        yb = _rwkv_scan(rw, _row(lnx_gain[l]), _row(lnx_bias[l]))

        yc = _short_conv(seq(pc), conv_w[l], _row(conv_out_norm[l]))

        h = _out_proj(h, ya.reshape(t, A_WIDTH), yb.reshape(t, B_WIDTH), yc.reshape(t, C_WIDTH),
                      w_out[l].astype(BF16))

        h = _ffn(h, _row(norm_ffn2[l]), ffn2_gate[l].astype(BF16), ffn2_up[l].astype(BF16),
                 ffn2_down[l].astype(BF16), g_final, l == DEPTH - 1)
    return h.reshape(b, s, d)
```

```python
import functools

import jax
import jax.numpy as jnp
from jax import lax
from jax.experimental import pallas as pl
from jax.experimental.pallas import tpu as pltpu

F32 = jnp.float32
BF16 = jnp.bfloat16

D_MODEL = 2048
DEPTH = 4
A_WIDTH = 1024
B_WIDTH = 512
C_WIDTH = 512
V_DIM = 128
NOPE_DIM = 128
ROPE_DIM = 64
QK_DIM = NOPE_DIM + ROPE_DIM
A_HEADS = 8
Q_LORA = 512
KV_LORA = 256
ROPE_THETA = 10000.0
B_HEAD = 64
B_HEADS = 8
DECAY_LORA = 32
ICLR_LORA = 32
GATE_LORA = 96
HEAD_NORM_EPS = 64e-5
C_GROUPS = 8
CONV_K = 3
D_FF = 5632
RMS_EPS = 1e-6
LOG2_E = 1.4426950408889634
A_IN = Q_LORA + KV_LORA + ROPE_DIM
B_IN = 3 * B_WIDTH + DECAY_LORA + ICLR_LORA + GATE_LORA

CKV_W = KV_LORA + 2 * ROPE_DIM
PBM_W = 3 * B_WIDTH
PBL_W = 256
PC_W = 3 * C_WIDTH
P_SPLITS = (Q_LORA, CKV_W, PBM_W, PBL_W, PC_W)
P_TOTAL = sum(P_SPLITS)

V7X_VMEM_BYTES = 64 * 1024 * 1024
SUBLANES = 8
LANES = 128
FFN_TM = 512
FFN_TF = 512
PROJ_TM = 512
SCAN_TILE = 256
PREP_TILE = 512
CUM_TILE = 256
ATTN_TQ = 256
CHUNK = 64
CHUNK_SHIFT = CHUNK.bit_length() - 1
INV_BASE_BITS = 3
ROPE_TW = 3 * LANES


def _cparams(sem, vmem_mb):
    return pltpu.CompilerParams(dimension_semantics=sem, vmem_limit_bytes=vmem_mb * 1024 * 1024)


def _rms(x, g, eps=RMS_EPS):
    return x * lax.rsqrt(jnp.mean(x * x, axis=-1, keepdims=True) + eps) * g


def _dot(a, b):
    return jnp.dot(a, b, preferred_element_type=F32)


def _dot_nt(a, b):
    return lax.dot_general(a, b, (((1,), (1,)), ((), ())), preferred_element_type=F32)


def _dot_tn(a, b):
    return lax.dot_general(a, b, (((0,), (0,)), ((), ())), preferred_element_type=F32)


def _split(x):
    hi = x.astype(BF16)
    lo = (x - hi.astype(F32)).astype(BF16)
    return hi, lo


def _dot_exact_rhs(x, m):
    hi, lo = _split(x)
    return _dot(hi, m) + _dot(lo, m)


def _dot3(x, w_hi, w_lo):
    hi, lo = _split(x)
    return _dot(hi, w_hi) + (_dot(lo, w_hi) + _dot(hi, w_lo))


def _shift_rows(prev8, x, k):
    cat = jnp.concatenate([prev8, x], axis=0)
    return pltpu.roll(cat, k, axis=0)[SUBLANES:]


def _group_ones(n, group):
    shift = group.bit_length() - 1
    r = lax.broadcasted_iota(jnp.int32, (n, n), 0) >> shift
    c = lax.broadcasted_iota(jnp.int32, (n, n), 1) >> shift
    return (r == c).astype(BF16)


def _rope_body(pos_ref, inv_ref, o_ref):
    ang = pos_ref[...].astype(F32) * inv_ref[...]
    c = jnp.cos(ang)
    s = jnp.sin(ang)
    lane = lax.broadcasted_iota(jnp.int32, ang.shape, 1)
    ssg = jnp.where((lane & (ROPE_DIM - 1)) < ROPE_DIM // 2, -s, s)
    o_ref[:, 0:LANES] = c
    o_ref[:, LANES:2 * LANES] = ssg
    o_ref[:, 2 * LANES:3 * LANES] = jnp.where(lane < ROPE_DIM, c, ssg)


def _rope_table(pos_col, inv_lane):
    t = pos_col.shape[0]
    tm = 1024
    return pl.pallas_call(
        _rope_body,
        out_shape=jax.ShapeDtypeStruct((t, ROPE_TW), F32),
        grid=(t // tm,),
        in_specs=[pl.BlockSpec((tm, 1), lambda i: (i, 0)),
                  pl.BlockSpec((1, LANES), lambda i: (0, 0))],
        out_specs=pl.BlockSpec((tm, ROPE_TW), lambda i: (i, 0)),
        compiler_params=_cparams(("parallel",), 16),
        name="rope_table",
    )(pos_col, inv_lane)


def _ffn_body(x_ref, g_ref, wg_ref, wu_ref, wd_ref, gf_ref, o_ref, xn_ref, *, final):
    j = pl.program_id(1)

    @pl.when(j == 0)
    def _():
        x = x_ref[...]
        xn_ref[...] = _rms(x, g_ref[...]).astype(BF16)
        o_ref[...] = x

    xn = xn_ref[...]
    gate = _dot(xn, wg_ref[...])
    up = _dot(xn, wu_ref[...])
    act = (gate * jax.nn.sigmoid(gate)) * (up * 0.5)
    o_ref[...] += _dot(act.astype(BF16), wd_ref[...])

    if final:
        @pl.when(j == pl.num_programs(1) - 1)
        def _():
            o_ref[...] = _rms(o_ref[...], gf_ref[...])


def _ffn(h, g, wg, wu, wd, g_final, final):
    t, d = h.shape
    f = wg.shape[1]
    tm, tf = FFN_TM, FFN_TF
    return pl.pallas_call(
        functools.partial(_ffn_body, final=final),
        out_shape=jax.ShapeDtypeStruct((t, d), F32),
        grid=(t // tm, f // tf),
        in_specs=[pl.BlockSpec((tm, d), lambda i, j: (i, 0)),
                  pl.BlockSpec((1, d), lambda i, j: (0, 0)),
                  pl.BlockSpec((d, tf), lambda i, j: (0, j)),
                  pl.BlockSpec((d, tf), lambda i, j: (0, j)),
                  pl.BlockSpec((tf, d), lambda i, j: (j, 0)),
                  pl.BlockSpec((1, d), lambda i, j: (0, 0))],
        out_specs=pl.BlockSpec((tm, d), lambda i, j: (i, 0)),
        scratch_shapes=[pltpu.VMEM((tm, d), BF16)],
        compiler_params=_cparams(("parallel", "arbitrary"), 48),
        name="ffn",
    )(h, g, wg, wu, wd, g_final)


def _inproj_body(x_ref, g_ref, w_ref, cq_ref, ckv_ref, pbm_ref, pbl_ref, pc_ref):
    xn = _rms(x_ref[...], g_ref[...]).astype(BF16)
    off = 0
    for ref in (cq_ref, ckv_ref, pbm_ref, pbl_ref, pc_ref):
        n = ref.shape[-1]
        ref[...] = _dot(xn, w_ref[:, off:off + n])
        off += n


def _in_proj(h, g, w):
    t, d = h.shape
    tm = PROJ_TM
    return pl.pallas_call(
        _inproj_body,
        out_shape=[jax.ShapeDtypeStruct((t, n), F32) for n in P_SPLITS],
        grid=(t // tm,),
        in_specs=[pl.BlockSpec((tm, d), lambda i: (i, 0)),
                  pl.BlockSpec((1, d), lambda i: (0, 0)),
                  pl.BlockSpec((d, P_TOTAL), lambda i: (0, 0), pipeline_mode=pl.Buffered(1))],
        out_specs=[pl.BlockSpec((tm, n), lambda i: (i, 0)) for n in P_SPLITS],
        compiler_params=_cparams(("parallel",), 56),
        name="in_proj",
    )(h, g, w)


def _mla_prep_body(cq_ref, ckv_ref, tab_ref, qn_ref, kvn_ref, wq_ref, wkv_ref, q_ref, k_ref, v_ref):
    nq = A_HEADS * NOPE_DIM
    nr = A_HEADS * ROPE_DIM
    scale = QK_DIM ** -0.5 * LOG2_E
    tab = tab_ref[0]
    cos2 = tab[:, 0:LANES]
    sin2 = tab[:, LANES:2 * LANES]
    cs = tab[:, 2 * LANES:3 * LANES]

    cqn = _rms(cq_ref[0], qn_ref[...]).astype(BF16)
    qa = _dot(cqn, wq_ref[...])
    cos_h = jnp.concatenate([cos2] * (nr // LANES), axis=1)
    sin_h = jnp.concatenate([sin2] * (nr // LANES), axis=1)
    q_rot = qa[:, nq:nq + nr] * cos_h + qa[:, nq + nr:nq + 2 * nr] * sin_h

    ckv = ckv_ref[0]
    ckvn = _rms(ckv[:, :KV_LORA], kvn_ref[...]).astype(BF16)
    kv = _dot(ckvn, wkv_ref[...])
    t = ckv[:, KV_LORA:KV_LORA + LANES] * cs
    k_rot = (t + pltpu.roll(t, ROPE_DIM, axis=1))[:, :ROPE_DIM].astype(BF16)

    for h in range(A_HEADS):
        q_ref[0, h, :, 0:NOPE_DIM] = (qa[:, h * NOPE_DIM:(h + 1) * NOPE_DIM] * scale).astype(BF16)
        q_ref[0, h, :, NOPE_DIM:QK_DIM] = (q_rot[:, h * ROPE_DIM:(h + 1) * ROPE_DIM] * scale).astype(BF16)
        k_ref[0, h, :, 0:NOPE_DIM] = kv[:, h * NOPE_DIM:(h + 1) * NOPE_DIM].astype(BF16)
        k_ref[0, h, :, NOPE_DIM:QK_DIM] = k_rot
        v_ref[0, h] = kv[:, nq + h * V_DIM:nq + (h + 1) * V_DIM].astype(BF16)


def _mla_prep(cq, ckv, tab, qn, kvn, wq, wkv):
    b, s, _ = cq.shape
    ts = PREP_TILE
    tok = lambda n: pl.BlockSpec((1, ts, n), lambda bi, i: (bi, i, 0))
    full = lambda a: pl.BlockSpec(a.shape, lambda bi, i: (0, 0))
    head = lambda n: pl.BlockSpec((1, A_HEADS, ts, n), lambda bi, i: (bi, 0, i, 0))
    return pl.pallas_call(
        _mla_prep_body,
        out_shape=[jax.ShapeDtypeStruct((b, A_HEADS, s, QK_DIM), BF16),
                   jax.ShapeDtypeStruct((b, A_HEADS, s, QK_DIM), BF16),
                   jax.ShapeDtypeStruct((b, A_HEADS, s, V_DIM), BF16)],
        grid=(b, s // ts),
        in_specs=[tok(Q_LORA), tok(CKV_W), tok(ROPE_TW), full(qn), full(kvn), full(wq), full(wkv)],
        out_specs=[head(QK_DIM), head(QK_DIM), head(V_DIM)],
        compiler_params=_cparams(("parallel", "parallel"), 48),
        name="mla_prep",
    )(cq, ckv, tab, qn, kvn, wq, wkv)


def _attn_body(q_ref, k_ref, v_ref, g_ref, o_ref, s_ref, mx_ref, m_ref, acc_ref):
    i = pl.program_id(1)
    nh = q_ref.shape[1]
    tq = q_ref.shape[2]
    heads = range(nh)
    ones = jnp.ones((tq, V_DIM), BF16)
    keys = lambda j: pl.ds(pl.multiple_of(j * tq, tq), tq)

    row = lax.broadcasted_iota(jnp.int32, (tq, tq), 0)
    col = lax.broadcasted_iota(jnp.int32, (tq, tq), 1)
    for h in heads:
        s = jnp.where(col <= row, _dot_nt(q_ref[0, h], k_ref[0, h, keys(i), :]), -1e30)
        s_ref[h, i] = s
        mx_ref[h] = s

    def scores(j, _):
        for h in heads:
            s = _dot_nt(q_ref[0, h], k_ref[0, h, keys(j), :])
            s_ref[h, j] = s
            mx_ref[h] = jnp.maximum(mx_ref[h], s)
        return 0

    lax.fori_loop(0, i, scores, 0)
    for h in heads:
        m_ref[h] = jnp.broadcast_to(jnp.max(mx_ref[h], axis=-1, keepdims=True), (tq, LANES))

    def weighted(j, h):
        m = m_ref[h]
        p = jnp.exp2(s_ref[h, j] - jnp.concatenate([m] * (tq // LANES), axis=1))
        return _dot(p.astype(BF16), jnp.concatenate([v_ref[0, h, keys(j), :], ones], axis=1))

    for h in heads:
        acc_ref[h] = weighted(i, h)

    def values(j, _):
        for h in heads:
            acc_ref[h] += weighted(j, h)
        return 0

    lax.fori_loop(0, i, values, 0)
    for h in heads:
        acc = acc_ref[h]
        hs = slice(h * V_DIM, (h + 1) * V_DIM)
        o_ref[0, :, hs] = _rms(acc[:, :V_DIM] / acc[:, V_DIM:], g_ref[:, hs]).astype(o_ref.dtype)


def _attention(q, k, v, g):
    b, nh, s, _ = q.shape
    tq = ATTN_TQ
    return pl.pallas_call(
        _attn_body,
        out_shape=jax.ShapeDtypeStruct((b, s, nh * V_DIM), BF16),
        grid=(b, s // tq),
        in_specs=[pl.BlockSpec((1, nh, tq, QK_DIM), lambda bi, i: (bi, 0, i, 0)),
                  pl.BlockSpec((1, nh, s, QK_DIM), lambda bi, i: (bi, 0, 0, 0)),
                  pl.BlockSpec((1, nh, s, V_DIM), lambda bi, i: (bi, 0, 0, 0)),
                  pl.BlockSpec((1, nh * V_DIM), lambda bi, i: (0, 0))],
        out_specs=pl.BlockSpec((1, tq, nh * V_DIM), lambda bi, i: (bi, i, 0)),
        scratch_shapes=[pltpu.VMEM((nh, s // tq, tq, tq), F32),
                        pltpu.VMEM((nh, tq, tq), F32),
                        pltpu.VMEM((nh, tq, LANES), F32),
                        pltpu.VMEM((nh, tq, 2 * V_DIM), F32)],
        compiler_params=_cparams(("parallel", "arbitrary"), 56),
        name="mla_attention",
    )(q, k, v, g)


def _rwkv_prep_body(pbm_ref, pbl_ref, pbm_prev_ref, pbl_prev_ref, mum_ref, mul_ref, w0_ref, a0_ref,
                    dec_hi_ref, dec_lo_ref, icl_hi_ref, icl_lo_ref, gat_hi_ref, gat_lo_ref,
                    kk_ref, ka_ref, rk_ref,
                    at_ref, rt_ref, bt_ref, kt_ref, bh_ref, kh_ref, v_ref, g_ref, bonus_ref, ptot_ref):
    i = pl.program_id(1)
    ts = pbm_ref.shape[1]
    w = B_WIDTH
    first = i == 0

    xm = pbm_ref[0]
    xl = pbl_ref[0]
    pm = jnp.where(first, 0.0, pbm_prev_ref[0])
    pv = jnp.where(first, 0.0, pbl_prev_ref[0])
    xs = xm + (_shift_rows(pm, xm, 1) - xm) * mum_ref[...]
    ls = xl + (_shift_rows(pv, xl, 1) - xl) * mul_ref[...]
    r = xs[:, 0:w]
    k = xs[:, w:2 * w]
    v = xs[:, 2 * w:3 * w]

    wl = w0_ref[...] + _dot3(jnp.tanh(ls), dec_hi_ref[...], dec_lo_ref[...])
    z = -wl
    softplus = jnp.maximum(z, 0.0) + jnp.log1p(jnp.exp(-jnp.abs(z)))
    lw = -jnp.exp(-softplus - 0.5)
    a_ic = jax.nn.sigmoid(a0_ref[...] + _dot3(ls, icl_hi_ref[...], icl_lo_ref[...]))
    gate = _dot3(jax.nn.sigmoid(ls), gat_hi_ref[...], gat_lo_ref[...])

    seg = _group_ones(w, B_HEAD)
    kk = k * kk_ref[...]
    kkn = kk / jnp.maximum(jnp.sqrt(_dot_exact_rhs(kk * kk, seg)), 1e-12)
    kp = k * (1.0 + (a_ic - 1.0) * ka_ref[...])
    bonus = _dot_exact_rhs(r * kp * rk_ref[...], seg) * v

    ct = CUM_TILE
    row = lax.broadcasted_iota(jnp.int32, (ct, ct), 0)
    col = lax.broadcasted_iota(jnp.int32, (ct, ct), 1)
    same = (row >> CHUNK_SHIFT) == (col >> CHUNK_SHIFT)
    lw_hi, lw_lo = _split(lw)
    tri = jnp.concatenate([(same & (col <= row)).astype(BF16), (same & (col > row)).astype(BF16)], axis=0)
    cums = [_dot(tri, lw_hi[t0:t0 + ct]) + _dot(tri, lw_lo[t0:t0 + ct]) for t0 in range(0, ts, ct)]
    cum = jnp.concatenate([c[:ct] for c in cums], axis=0)
    cumr = jnp.concatenate([c[ct:] for c in cums], axis=0)

    e_neg = jnp.exp(-cum)
    e_rem = jnp.exp(cumr)
    b_vec = kkn * a_ic
    at_ref[0] = (-kkn * jnp.exp(cum - lw)).astype(BF16)
    rt_ref[0] = (r * jnp.exp(cum)).astype(BF16)
    bt_ref[0] = (b_vec * e_neg).astype(BF16)
    kt_ref[0] = (kp * e_neg).astype(BF16)
    bh_ref[0] = (b_vec * e_rem).astype(BF16)
    kh_ref[0] = (kp * e_rem).astype(BF16)
    v_ref[0] = v.astype(BF16)
    g_ref[0] = gate
    bonus_ref[0] = bonus
    nc = ptot_ref.shape[2]
    for t in range(ptot_ref.shape[1]):
        last_rows = [(t * nc + c + 1) * CHUNK - 1 for c in range(nc)]
        ptot_ref[0, t] = jnp.exp(jnp.concatenate([cum[r0:r0 + 1] for r0 in last_rows], axis=0))


def _rwkv_prep(pbm, pbl, params):
    b, s, _ = pbm.shape
    ts = PREP_TILE
    nc = SCAN_TILE // CHUNK
    nt = ts // SCAN_TILE
    w = B_WIDTH
    tok = lambda n: pl.BlockSpec((1, ts, n), lambda bi, i: (bi, i, 0))
    prev = lambda n: pl.BlockSpec(
        (1, SUBLANES, n), lambda bi, i: (bi, jnp.maximum(i * (ts // SUBLANES) - 1, 0), 0))
    full = lambda a: pl.BlockSpec(a.shape, lambda bi, i: (0, 0))
    return pl.pallas_call(
        _rwkv_prep_body,
        out_shape=[jax.ShapeDtypeStruct((b, s, w), BF16)] * 7
        + [jax.ShapeDtypeStruct((b, s, w), F32)] * 2
        + [jax.ShapeDtypeStruct((b, s // SCAN_TILE, nc, w), F32)],
        grid=(b, s // ts),
        in_specs=[tok(PBM_W), tok(PBL_W), prev(PBM_W), prev(PBL_W)] + [full(p) for p in params],
        out_specs=[tok(w)] * 9 + [pl.BlockSpec((1, nt, nc, w), lambda bi, i: (bi, i, 0, 0))],
        compiler_params=_cparams(("parallel", "parallel"), 48),
        name="rwkv_prep",
    )(pbm, pbl, pbm, pbl, *params)


def _rwkv_scan_body(at_ref, rt_ref, bt_ref, kt_ref, bh_ref, kh_ref, v_ref, g_ref, bonus_ref, ptot_ref,
                    gain_ref, bias_ref, o_ref, state_ref):
    n = B_HEAD
    c_len = CHUNK
    ts = at_ref.shape[1]

    @pl.when(pl.program_id(1) == 0)
    def _():
        state_ref[...] = jnp.zeros_like(state_ref)

    row = lax.broadcasted_iota(jnp.int32, (c_len, 2 * c_len), 0)
    col = lax.broadcasted_iota(jnp.int32, (c_len, 2 * c_len), 1) & (c_len - 1)
    strict2 = col < row
    incl2 = col <= row
    eye = (lax.broadcasted_iota(jnp.int32, (n, n), 0) == lax.broadcasted_iota(jnp.int32, (n, n), 1))
    eye_f = eye.astype(F32)
    zero_blk = jnp.zeros((c_len, n), BF16)
    sq_row = lax.broadcasted_iota(jnp.int32, (c_len, c_len), 0)
    sq_col = lax.broadcasted_iota(jnp.int32, (c_len, c_len), 1)
    same_block = lambda bits: (sq_row >> bits) == (sq_col >> bits)
    diag_blocks = same_block(INV_BASE_BITS)
    merge_masks = [same_block(bits + 1) & ~same_block(bits) for bits in range(INV_BASE_BITS, CHUNK_SHIFT)]

    n_chunks = ts // c_len
    heads = range(at_ref.shape[2] // n)
    units = [(c, h) for c in range(n_chunks) for h in heads]
    rows = lambda c: slice(c * c_len, (c + 1) * c_len)
    lanes = lambda h: slice(h * n, (h + 1) * n)
    load = lambda ref: {(c, h): ref[0, rows(c), lanes(h)] for c, h in units}
    at, rt, bt, kt, bh, kh, vv = (load(r) for r in (at_ref, rt_ref, bt_ref, kt_ref, bh_ref, kh_ref, v_ref))
    state = [state_ref[h] for h in heads]

    g1 = {u: _dot_nt(jnp.concatenate([at[u], rt[u]], axis=0), jnp.concatenate([bt[u], kt[u]], axis=0))
          for u in units}
    a_low = {u: jnp.where(strict2, g1[u][:c_len], 0.0) for u in units}
    r_low = {u: jnp.where(incl2, g1[u][c_len:], 0.0).astype(BF16) for u in units}

    l_mat = {u: a_low[u][:, :c_len] for u in units}
    p = {u: jnp.where(diag_blocks, l_mat[u], 0.0) for u in units}
    t_inv = {u: eye_f + p[u] for u in units}
    p = {u: p[u].astype(BF16) for u in units}
    for _ in range(INV_BASE_BITS - 1):
        p = {u: _dot(p[u], p[u]).astype(BF16) for u in units}
        t_inv = {u: t_inv[u] + _dot(t_inv[u].astype(BF16), p[u]) for u in units}
    for off_mask in merge_masks:
        t_b = {u: t_inv[u].astype(BF16) for u in units}
        t_l = {u: _dot(t_b[u], jnp.where(off_mask, l_mat[u], 0.0).astype(BF16)).astype(BF16) for u in units}
        t_inv = {u: t_inv[u] + _dot(t_l[u], t_b[u]) for u in units}

    akv = {u: _dot(a_low[u][:, c_len:].astype(BF16), vv[u]).astype(BF16) for u in units}
    x = {u: _dot(t_inv[u].astype(BF16), jnp.concatenate([at[u], akv[u]], axis=1)).astype(BF16) for u in units}
    rhs = {u: jnp.concatenate([x[u], jnp.concatenate([zero_blk, vv[u]], axis=1)], axis=0) for u in units}
    qy = {u: _dot(r_low[u], rhs[u]) for u in units}
    wmat = {u: _dot_tn(rhs[u], jnp.concatenate([bh[u], kh[u]], axis=0)) for u in units}
    mc = {(c, h): (jnp.where(eye, ptot_ref[0, 0, c:c + 1, lanes(h)], 0.0) + wmat[(c, h)][:n]).astype(BF16)
          for c, h in units}
    rp = {u: (rt[u].astype(F32) + qy[u][:, :n]).astype(BF16) for u in units}

    y = {}
    for c, h in units:
        s0 = state[h].astype(BF16)
        y[(c, h)] = _dot_nt(rp[(c, h)], s0) + qy[(c, h)][:, n:]
        state[h] = _dot(s0, mc[(c, h)]) + wmat[(c, h)][n:]

    for c in range(n_chunks):
        outs = []
        for h in heads:
            yv = y[(c, h)]
            mu = jnp.mean(yv, axis=-1, keepdims=True)
            yc = yv - mu
            var = jnp.mean(yc * yc, axis=-1, keepdims=True)
            yn = yc * lax.rsqrt(var + HEAD_NORM_EPS) * gain_ref[:, lanes(h)] + bias_ref[:, lanes(h)]
            outs.append((yn + bonus_ref[0, rows(c), lanes(h)]) * g_ref[0, rows(c), lanes(h)])
        o_ref[0, rows(c), :] = jnp.concatenate(outs, axis=1).astype(o_ref.dtype)
    for h in heads:
        state_ref[h] = state[h]


def _rwkv_scan(arrs, gain, bias):
    at = arrs[0]
    b, s, w = at.shape
    ts = SCAN_TILE
    nc = ts // CHUNK
    tok = pl.BlockSpec((1, ts, w), lambda bi, i: (bi, i, 0))
    vec = pl.BlockSpec((1, w), lambda bi, i: (0, 0))
    return pl.pallas_call(
        _rwkv_scan_body,
        out_shape=jax.ShapeDtypeStruct((b, s, w), BF16),
        grid=(b, s // ts),
        in_specs=[tok] * 9 + [pl.BlockSpec((1, 1, nc, w), lambda bi, i: (bi, i, 0, 0)), vec, vec],
        out_specs=tok,
        scratch_shapes=[pltpu.VMEM((w // B_HEAD, B_HEAD, B_HEAD), F32)],
        compiler_params=_cparams(("parallel", "arbitrary"), 32),
        name="rwkv_scan",
    )(*arrs, gain, bias)


def _conv_body(pc_ref, prev_ref, w_ref, g_ref, o_ref):
    w = C_WIDTH
    x = pc_ref[0]
    xp = jnp.where(pl.program_id(1) == 0, 0.0, prev_ref[0])
    u = x[:, w:2 * w] * x[:, 2 * w:3 * w]
    up = xp[:, w:2 * w] * xp[:, 2 * w:3 * w]
    cw = w_ref[...]
    y = cw[0:1] * _shift_rows(up, u, 2) + cw[1:2] * _shift_rows(up, u, 1) + cw[2:3] * u
    z = x[:, 0:w] * y
    ms = _dot_exact_rhs(z * z, _group_ones(w, w // C_GROUPS)) * (C_GROUPS / w)
    o_ref[0] = (z * lax.rsqrt(ms + RMS_EPS) * g_ref[...]).astype(o_ref.dtype)


def _short_conv(pc, conv_w, gain):
    b, s, _ = pc.shape
    ts = PREP_TILE
    return pl.pallas_call(
        _conv_body,
        out_shape=jax.ShapeDtypeStruct((b, s, C_WIDTH), BF16),
        grid=(b, s // ts),
        in_specs=[pl.BlockSpec((1, ts, PC_W), lambda bi, i: (bi, i, 0)),
                  pl.BlockSpec((1, SUBLANES, PC_W),
                               lambda bi, i: (bi, jnp.maximum(i * (ts // SUBLANES) - 1, 0), 0)),
                  pl.BlockSpec(conv_w.shape, lambda bi, i: (0, 0)),
                  pl.BlockSpec(gain.shape, lambda bi, i: (0, 0))],
        out_specs=pl.BlockSpec((1, ts, C_WIDTH), lambda bi, i: (bi, i, 0)),
        compiler_params=_cparams(("parallel", "parallel"), 32),
        name="short_conv",
    )(pc, pc, conv_w, gain)


def _outproj_body(h_ref, ya_ref, yb_ref, yc_ref, w_ref, o_ref):
    acc = h_ref[...] + _dot(ya_ref[...], w_ref[0:A_WIDTH, :])
    acc = acc + _dot(yb_ref[...], w_ref[A_WIDTH:A_WIDTH + B_WIDTH, :])
    o_ref[...] = acc + _dot(yc_ref[...], w_ref[A_WIDTH + B_WIDTH:, :])


def _out_proj(h, ya, yb, yc, w):
    t, d = h.shape
    tm = PROJ_TM
    row = lambda n: pl.BlockSpec((tm, n), lambda i: (i, 0))
    return pl.pallas_call(
        _outproj_body,
        out_shape=jax.ShapeDtypeStruct((t, d), F32),
        grid=(t // tm,),
        in_specs=[row(d), row(A_WIDTH), row(B_WIDTH), row(C_WIDTH),
                  pl.BlockSpec(w.shape, lambda i: (0, 0))],
        out_specs=row(d),
        compiler_params=_cparams(("parallel",), 48),
        name="out_proj",
    )(h, ya, yb, yc, w)


def _swap_halves(w):
    half = ROPE_DIM // 2
    return jnp.concatenate([w[..., half:], w[..., :half]], axis=-1)


def _arrange_w_in(w_in):
    w_in = w_in.astype(BF16)
    d = w_in.shape[0]
    o = Q_LORA + KV_LORA
    k_rope = w_in[:, o:A_IN]
    b0 = A_IN
    lora = w_in[:, b0 + PBM_W:b0 + B_IN]
    pad = jnp.zeros((d, PBL_W - lora.shape[1]), w_in.dtype)
    return jnp.concatenate(
        [w_in[:, :Q_LORA], w_in[:, Q_LORA:o], k_rope, _swap_halves(k_rope),
         w_in[:, b0:b0 + PBM_W], lora, pad, w_in[:, b0 + B_IN:]], axis=1)


def _arrange_w_uq(w_uq):
    w = w_uq.astype(BF16).reshape(Q_LORA, A_HEADS, QK_DIM)
    nope = w[:, :, :NOPE_DIM].reshape(Q_LORA, -1)
    rope = w[:, :, NOPE_DIM:]
    return jnp.concatenate(
        [nope, rope.reshape(Q_LORA, -1), _swap_halves(rope).reshape(Q_LORA, -1)], axis=1)


def _arrange_w_ukv(w_ukv):
    w = w_ukv.astype(BF16).reshape(KV_LORA, A_HEADS, NOPE_DIM + V_DIM)
    return jnp.concatenate(
        [w[:, :, :NOPE_DIM].reshape(KV_LORA, -1), w[:, :, NOPE_DIM:].reshape(KV_LORA, -1)], axis=1)


def _pad_rows(w, start):
    full = jnp.zeros((PBL_W, w.shape[1]), F32).at[start:start + w.shape[0]].set(w)
    hi = full.astype(BF16)
    return hi, (full - hi.astype(F32)).astype(BF16)


def _row(v):
    return v.reshape(1, -1)


def kernel(x, positions, norm_ffn1, ffn1_gate, ffn1_up, ffn1_down, norm_mix, w_in, q_norm, kv_norm, w_uq, w_ukv, attn_out_norm, shift_mu, decay_w0, decay_up, iclr_a0, iclr_up, gate_up, k_k, k_a, r_k, lnx_gain, lnx_bias, conv_w, conv_out_norm, w_out, norm_ffn2, ffn2_gate, ffn2_up, ffn2_down, norm_final):
    b, s, d = x.shape
    t = b * s
    seq = lambda a: a.reshape(b, s, a.shape[-1])

    inv_freq = 1.0 / (ROPE_THETA ** (jnp.arange(0, ROPE_DIM, 2, dtype=F32) / ROPE_DIM))
    inv_lane = jnp.tile(inv_freq, LANES // inv_freq.shape[0]).reshape(1, LANES)
    tab = seq(_rope_table(positions.reshape(t, 1), inv_lane))

    h = x.reshape(t, d)
    g_final = _row(norm_final)
    for l in range(DEPTH):
        h = _ffn(h, _row(norm_ffn1[l]), ffn1_gate[l].astype(BF16), ffn1_up[l].astype(BF16),
                 ffn1_down[l].astype(BF16), g_final, False)

        cq, ckv, pbm, pbl, pc = _in_proj(h, _row(norm_mix[l]), _arrange_w_in(w_in[l]))

        q, k, v = _mla_prep(seq(cq), seq(ckv), tab, _row(q_norm[l]), _row(kv_norm[l]),
                            _arrange_w_uq(w_uq[l]), _arrange_w_ukv(w_ukv[l]))
        ya = _attention(q, k, v, _row(attn_out_norm[l]))

        mu = shift_mu[l]
        mu_l = jnp.zeros((PBL_W,), F32).at[:B_IN - PBM_W].set(mu[PBM_W:])
        rw_params = (_row(mu[:PBM_W]), _row(mu_l), _row(decay_w0[l]), _row(iclr_a0[l]),
                     *_pad_rows(decay_up[l], 0), *_pad_rows(iclr_up[l], DECAY_LORA),
                     *_pad_rows(gate_up[l], DECAY_LORA + ICLR_LORA),
                     _row(k_k[l]), _row(k_a[l]), _row(r_k[l]))
        rw = _rwkv_prep(seq(pbm), seq(pbl), rw_params)
        yb = _rwkv_scan(rw, _row(lnx_gain[l]), _row(lnx_bias[l]))

        yc = _short_conv(seq(pc), conv_w[l], _row(conv_out_norm[l]))

        h = _out_proj(h, ya.reshape(t, A_WIDTH), yb.reshape(t, B_WIDTH), yc.reshape(t, C_WIDTH),
                      w_out[l].astype(BF16))

        h = _ffn(h, _row(norm_ffn2[l]), ffn2_gate[l].astype(BF16), ffn2_up[l].astype(BF16),
                 ffn2_down[l].astype(BF16), g_final, l == DEPTH - 1)
    return h.reshape(b, s, d)
```

```python
import functools

import jax
import jax.numpy as jnp
from jax import lax
from jax.experimental import pallas as pl
from jax.experimental.pallas import tpu as pltpu

F32 = jnp.float32
BF16 = jnp.bfloat16

D_MODEL = 2048
DEPTH = 4
A_WIDTH = 1024
B_WIDTH = 512
C_WIDTH = 512
V_DIM = 128
NOPE_DIM = 128
ROPE_DIM = 64
QK_DIM = NOPE_DIM + ROPE_DIM
A_HEADS = 8
Q_LORA = 512
KV_LORA = 256
ROPE_THETA = 10000.0
B_HEAD = 64
B_HEADS = 8
DECAY_LORA = 32
ICLR_LORA = 32
GATE_LORA = 96
HEAD_NORM_EPS = 64e-5
C_GROUPS = 8
CONV_K = 3
D_FF = 5632
RMS_EPS = 1e-6
LOG2_E = 1.4426950408889634
A_IN = Q_LORA + KV_LORA + ROPE_DIM
B_IN = 3 * B_WIDTH + DECAY_LORA + ICLR_LORA + GATE_LORA

CKV_W = KV_LORA + 2 * ROPE_DIM
PBM_W = 3 * B_WIDTH
PBL_W = 256
PC_W = 3 * C_WIDTH
P_SPLITS = (Q_LORA, CKV_W, PBM_W, PBL_W, PC_W)
P_TOTAL = sum(P_SPLITS)

V7X_VMEM_BYTES = 64 * 1024 * 1024
SUBLANES = 8
LANES = 128
FFN_TM = 1024
FFN_TF = 512
PROJ_TM = 512
SCAN_TILE = 256
PREP_TILE = 512
CUM_TILE = 256
ATTN_TQ = 256
CHUNK = 64
CHUNK_SHIFT = CHUNK.bit_length() - 1
INV_BASE_BITS = 3
ROPE_TW = 3 * LANES


def _cparams(sem, vmem_mb):
    return pltpu.CompilerParams(dimension_semantics=sem, vmem_limit_bytes=vmem_mb * 1024 * 1024)


def _rms(x, g, eps=RMS_EPS):
    return x * lax.rsqrt(jnp.mean(x * x, axis=-1, keepdims=True) + eps) * g


def _dot(a, b):
    return jnp.dot(a, b, preferred_element_type=F32)


def _dot_nt(a, b):
    return lax.dot_general(a, b, (((1,), (1,)), ((), ())), preferred_element_type=F32)


def _dot_tn(a, b):
    return lax.dot_general(a, b, (((0,), (0,)), ((), ())), preferred_element_type=F32)


def _split(x):
    hi = x.astype(BF16)
    lo = (x - hi.astype(F32)).astype(BF16)
    return hi, lo


def _dot_exact_rhs(x, m):
    hi, lo = _split(x)
    return _dot(hi, m) + _dot(lo, m)


def _dot3(x, w_hi, w_lo):
    hi, lo = _split(x)
    return _dot(hi, w_hi) + (_dot(lo, w_hi) + _dot(hi, w_lo))


def _shift_rows(prev8, x, k):
    cat = jnp.concatenate([prev8, x], axis=0)
    return pltpu.roll(cat, k, axis=0)[SUBLANES:]


def _group_ones(n, group):
    shift = group.bit_length() - 1
    r = lax.broadcasted_iota(jnp.int32, (n, n), 0) >> shift
    c = lax.broadcasted_iota(jnp.int32, (n, n), 1) >> shift
    return (r == c).astype(BF16)


def _rope_body(pos_ref, inv_ref, o_ref):
    ang = pos_ref[...].astype(F32) * inv_ref[...]
    c = jnp.cos(ang)
    s = jnp.sin(ang)
    lane = lax.broadcasted_iota(jnp.int32, ang.shape, 1)
    ssg = jnp.where((lane & (ROPE_DIM - 1)) < ROPE_DIM // 2, -s, s)
    o_ref[:, 0:LANES] = c
    o_ref[:, LANES:2 * LANES] = ssg
    o_ref[:, 2 * LANES:3 * LANES] = jnp.where(lane < ROPE_DIM, c, ssg)


def _rope_table(pos_col, inv_lane):
    t = pos_col.shape[0]
    tm = 1024
    return pl.pallas_call(
        _rope_body,
        out_shape=jax.ShapeDtypeStruct((t, ROPE_TW), F32),
        grid=(t // tm,),
        in_specs=[pl.BlockSpec((tm, 1), lambda i: (i, 0)),
                  pl.BlockSpec((1, LANES), lambda i: (0, 0))],
        out_specs=pl.BlockSpec((tm, ROPE_TW), lambda i: (i, 0)),
        compiler_params=_cparams(("parallel",), 16),
        name="rope_table",
    )(pos_col, inv_lane)


def _ffn_body(x_ref, g_ref, wg_ref, wu_ref, wd_ref, gf_ref, o_ref, xn_ref, *, final):
    j = pl.program_id(1)

    @pl.when(j == 0)
    def _():
        x = x_ref[...]
        xn_ref[...] = _rms(x, g_ref[...]).astype(BF16)
        o_ref[...] = x

    xn = xn_ref[...]
    gate = _dot(xn, wg_ref[...])
    up = _dot(xn, wu_ref[...])
    act = (gate * jax.nn.sigmoid(gate)) * (up * 0.5)
    o_ref[...] += _dot(act.astype(BF16), wd_ref[...])

    if final:
        @pl.when(j == pl.num_programs(1) - 1)
        def _():
            o_ref[...] = _rms(o_ref[...], gf_ref[...])


def _ffn(h, g, wg, wu, wd, layer, g_final, final):
    t, d = h.shape
    f = wg.shape[2]
    tm, tf = FFN_TM, FFN_TF
    return pl.pallas_call(
        functools.partial(_ffn_body, final=final),
        out_shape=jax.ShapeDtypeStruct((t, d), F32),
        grid=(t // tm, f // tf),
        in_specs=[pl.BlockSpec((tm, d), lambda i, j: (i, 0)),
                  pl.BlockSpec((1, d), lambda i, j: (0, 0)),
                  pl.BlockSpec((None, d, tf), lambda i, j: (layer, 0, j)),
                  pl.BlockSpec((None, d, tf), lambda i, j: (layer, 0, j)),
                  pl.BlockSpec((None, tf, d), lambda i, j: (layer, j, 0)),
                  pl.BlockSpec((1, d), lambda i, j: (0, 0))],
        out_specs=pl.BlockSpec((tm, d), lambda i, j: (i, 0)),
        scratch_shapes=[pltpu.VMEM((tm, d), BF16)],
        compiler_params=_cparams(("parallel", "arbitrary"), 60),
        name="ffn",
    )(h, g, wg, wu, wd, g_final)


def _inproj_body(x_ref, g_ref, w_ref, cq_ref, ckv_ref, pbm_ref, pbl_ref, pc_ref):
    xn = _rms(x_ref[...], g_ref[...]).astype(BF16)
    off = 0
    for ref in (cq_ref, ckv_ref, pbm_ref, pbl_ref, pc_ref):
        n = ref.shape[-1]
        ref[...] = _dot(xn, w_ref[:, off:off + n])
        off += n


def _in_proj(h, g, w):
    t, d = h.shape
    tm = PROJ_TM
    return pl.pallas_call(
        _inproj_body,
        out_shape=[jax.ShapeDtypeStruct((t, n), F32) for n in P_SPLITS],
        grid=(t // tm,),
        in_specs=[pl.BlockSpec((tm, d), lambda i: (i, 0)),
                  pl.BlockSpec((1, d), lambda i: (0, 0)),
                  pl.BlockSpec((d, P_TOTAL), lambda i: (0, 0), pipeline_mode=pl.Buffered(1))],
        out_specs=[pl.BlockSpec((tm, n), lambda i: (i, 0)) for n in P_SPLITS],
        compiler_params=_cparams(("parallel",), 56),
        name="in_proj",
    )(h, g, w)


def _mla_prep_body(cq_ref, ckv_ref, tab_ref, qn_ref, kvn_ref, wq_ref, wkv_ref, q_ref, k_ref, v_ref):
    nq = A_HEADS * NOPE_DIM
    nr = A_HEADS * ROPE_DIM
    scale = QK_DIM ** -0.5 * LOG2_E
    tab = tab_ref[0]
    cos2 = tab[:, 0:LANES]
    sin2 = tab[:, LANES:2 * LANES]
    cs = tab[:, 2 * LANES:3 * LANES]

    cqn = _rms(cq_ref[0], qn_ref[...]).astype(BF16)
    qa = _dot(cqn, wq_ref[...])
    cos_h = jnp.concatenate([cos2] * (nr // LANES), axis=1)
    sin_h = jnp.concatenate([sin2] * (nr // LANES), axis=1)
    q_rot = qa[:, nq:nq + nr] * cos_h + qa[:, nq + nr:nq + 2 * nr] * sin_h

    ckv = ckv_ref[0]
    ckvn = _rms(ckv[:, :KV_LORA], kvn_ref[...]).astype(BF16)
    kv = _dot(ckvn, wkv_ref[...])
    t = ckv[:, KV_LORA:KV_LORA + LANES] * cs
    k_rot = (t + pltpu.roll(t, ROPE_DIM, axis=1))[:, :ROPE_DIM].astype(BF16)

    for h in range(A_HEADS):
        q_ref[0, h, :, 0:NOPE_DIM] = (qa[:, h * NOPE_DIM:(h + 1) * NOPE_DIM] * scale).astype(BF16)
        q_ref[0, h, :, NOPE_DIM:QK_DIM] = (q_rot[:, h * ROPE_DIM:(h + 1) * ROPE_DIM] * scale).astype(BF16)
        k_ref[0, h, :, 0:NOPE_DIM] = kv[:, h * NOPE_DIM:(h + 1) * NOPE_DIM].astype(BF16)
        k_ref[0, h, :, NOPE_DIM:QK_DIM] = k_rot
        v_ref[0, h] = kv[:, nq + h * V_DIM:nq + (h + 1) * V_DIM].astype(BF16)


def _mla_prep(cq, ckv, tab, qn, kvn, wq, wkv):
    b, s, _ = cq.shape
    ts = PREP_TILE
    tok = lambda n: pl.BlockSpec((1, ts, n), lambda bi, i: (bi, i, 0))
    full = lambda a: pl.BlockSpec(a.shape, lambda bi, i: (0, 0))
    head = lambda n: pl.BlockSpec((1, A_HEADS, ts, n), lambda bi, i: (bi, 0, i, 0))
    return pl.pallas_call(
        _mla_prep_body,
        out_shape=[jax.ShapeDtypeStruct((b, A_HEADS, s, QK_DIM), BF16),
                   jax.ShapeDtypeStruct((b, A_HEADS, s, QK_DIM), BF16),
                   jax.ShapeDtypeStruct((b, A_HEADS, s, V_DIM), BF16)],
        grid=(b, s // ts),
        in_specs=[tok(Q_LORA), tok(CKV_W), tok(ROPE_TW), full(qn), full(kvn), full(wq), full(wkv)],
        out_specs=[head(QK_DIM), head(QK_DIM), head(V_DIM)],
        compiler_params=_cparams(("parallel", "parallel"), 48),
        name="mla_prep",
    )(cq, ckv, tab, qn, kvn, wq, wkv)


def _attn_body(q_ref, k_ref, v_ref, g_ref, o_ref, s_ref, mx_ref, m_ref, acc_ref):
    i = pl.program_id(1)
    nh = q_ref.shape[1]
    tq = q_ref.shape[2]
    heads = range(nh)
    ones = jnp.ones((tq, V_DIM), BF16)
    keys = lambda j: pl.ds(pl.multiple_of(j * tq, tq), tq)

    row = lax.broadcasted_iota(jnp.int32, (tq, tq), 0)
    col = lax.broadcasted_iota(jnp.int32, (tq, tq), 1)
    for h in heads:
        s = jnp.where(col <= row, _dot_nt(q_ref[0, h], k_ref[0, h, keys(i), :]), -1e30)
        s_ref[h, i] = s
        mx_ref[h] = s

    def scores(j, _):
        for h in heads:
            s = _dot_nt(q_ref[0, h], k_ref[0, h, keys(j), :])
            s_ref[h, j] = s
            mx_ref[h] = jnp.maximum(mx_ref[h], s)
        return 0

    lax.fori_loop(0, i, scores, 0)
    for h in heads:
        m_ref[h] = jnp.broadcast_to(jnp.max(mx_ref[h], axis=-1, keepdims=True), (tq, LANES))

    def weighted(j, h):
        m = m_ref[h]
        p = jnp.exp2(s_ref[h, j] - jnp.concatenate([m] * (tq // LANES), axis=1))
        return _dot(p.astype(BF16), jnp.concatenate([v_ref[0, h, keys(j), :], ones], axis=1))

    for h in heads:
        acc_ref[h] = weighted(i, h)

    def values(j, _):
        for h in heads:
            acc_ref[h] += weighted(j, h)
        return 0

    lax.fori_loop(0, i, values, 0)
    for h in heads:
        acc = acc_ref[h]
        hs = slice(h * V_DIM, (h + 1) * V_DIM)
        o_ref[0, :, hs] = _rms(acc[:, :V_DIM] / acc[:, V_DIM:], g_ref[:, hs]).astype(o_ref.dtype)


def _attention(q, k, v, g):
    b, nh, s, _ = q.shape
    tq = ATTN_TQ
    return pl.pallas_call(
        _attn_body,
        out_shape=jax.ShapeDtypeStruct((b, s, nh * V_DIM), BF16),
        grid=(b, s // tq),
        in_specs=[pl.BlockSpec((1, nh, tq, QK_DIM), lambda bi, i: (bi, 0, i, 0)),
                  pl.BlockSpec((1, nh, s, QK_DIM), lambda bi, i: (bi, 0, 0, 0)),
                  pl.BlockSpec((1, nh, s, V_DIM), lambda bi, i: (bi, 0, 0, 0)),
                  pl.BlockSpec((1, nh * V_DIM), lambda bi, i: (0, 0))],
        out_specs=pl.BlockSpec((1, tq, nh * V_DIM), lambda bi, i: (bi, i, 0)),
        scratch_shapes=[pltpu.VMEM((nh, s // tq, tq, tq), F32),
                        pltpu.VMEM((nh, tq, tq), F32),
                        pltpu.VMEM((nh, tq, LANES), F32),
                        pltpu.VMEM((nh, tq, 2 * V_DIM), F32)],
        compiler_params=_cparams(("parallel", "arbitrary"), 56),
        name="mla_attention",
    )(q, k, v, g)


def _rwkv_prep_body(pbm_ref, pbl_ref, pbm_prev_ref, pbl_prev_ref, mum_ref, mul_ref, w0_ref, a0_ref,
                    dec_hi_ref, dec_lo_ref, icl_hi_ref, icl_lo_ref, gat_hi_ref, gat_lo_ref,
                    kk_ref, ka_ref, rk_ref,
                    at_ref, rt_ref, bt_ref, kt_ref, bh_ref, kh_ref, v_ref, g_ref, bonus_ref, ptot_ref):
    i = pl.program_id(1)
    ts = pbm_ref.shape[1]
    w = B_WIDTH
    first = i == 0

    xm = pbm_ref[0]
    xl = pbl_ref[0]
    pm = jnp.where(first, 0.0, pbm_prev_ref[0])
    pv = jnp.where(first, 0.0, pbl_prev_ref[0])
    xs = xm + (_shift_rows(pm, xm, 1) - xm) * mum_ref[...]
    ls = xl + (_shift_rows(pv, xl, 1) - xl) * mul_ref[...]
    r = xs[:, 0:w]
    k = xs[:, w:2 * w]
    v = xs[:, 2 * w:3 * w]

    wl = w0_ref[...] + _dot3(jnp.tanh(ls), dec_hi_ref[...], dec_lo_ref[...])
    z = -wl
    softplus = jnp.maximum(z, 0.0) + jnp.log1p(jnp.exp(-jnp.abs(z)))
    lw = -jnp.exp(-softplus - 0.5)
    a_ic = jax.nn.sigmoid(a0_ref[...] + _dot3(ls, icl_hi_ref[...], icl_lo_ref[...]))
    gate = _dot3(jax.nn.sigmoid(ls), gat_hi_ref[...], gat_lo_ref[...])

    seg = _group_ones(w, B_HEAD)
    kk = k * kk_ref[...]
    kkn = kk / jnp.maximum(jnp.sqrt(_dot_exact_rhs(kk * kk, seg)), 1e-12)
    kp = k * (1.0 + (a_ic - 1.0) * ka_ref[...])
    bonus = _dot_exact_rhs(r * kp * rk_ref[...], seg) * v

    ct = CUM_TILE
    row = lax.broadcasted_iota(jnp.int32, (ct, ct), 0)
    col = lax.broadcasted_iota(jnp.int32, (ct, ct), 1)
    same = (row >> CHUNK_SHIFT) == (col >> CHUNK_SHIFT)
    lw_hi, lw_lo = _split(lw)
    tri = jnp.concatenate([(same & (col <= row)).astype(BF16), (same & (col > row)).astype(BF16)], axis=0)
    cums = [_dot(tri, lw_hi[t0:t0 + ct]) + _dot(tri, lw_lo[t0:t0 + ct]) for t0 in range(0, ts, ct)]
    cum = jnp.concatenate([c[:ct] for c in cums], axis=0)
    cumr = jnp.concatenate([c[ct:] for c in cums], axis=0)

    e_neg = jnp.exp(-cum)
    e_rem = jnp.exp(cumr)
    b_vec = kkn * a_ic
    at_ref[0] = (-kkn * jnp.exp(cum - lw)).astype(BF16)
    rt_ref[0] = (r * jnp.exp(cum)).astype(BF16)
    bt_ref[0] = (b_vec * e_neg).astype(BF16)
    kt_ref[0] = (kp * e_neg).astype(BF16)
    bh_ref[0] = (b_vec * e_rem).astype(BF16)
    kh_ref[0] = (kp * e_rem).astype(BF16)
    v_ref[0] = v.astype(BF16)
    g_ref[0] = gate
    bonus_ref[0] = bonus
    nc = ptot_ref.shape[2]
    for t in range(ptot_ref.shape[1]):
        last_rows = [(t * nc + c + 1) * CHUNK - 1 for c in range(nc)]
        ptot_ref[0, t] = jnp.exp(jnp.concatenate([cum[r0:r0 + 1] for r0 in last_rows], axis=0))


def _rwkv_prep(pbm, pbl, params):
    b, s, _ = pbm.shape
    ts = PREP_TILE
    nc = SCAN_TILE // CHUNK
    nt = ts // SCAN_TILE
    w = B_WIDTH
    tok = lambda n: pl.BlockSpec((1, ts, n), lambda bi, i: (bi, i, 0))
    prev = lambda n: pl.BlockSpec(
        (1, SUBLANES, n), lambda bi, i: (bi, jnp.maximum(i * (ts // SUBLANES) - 1, 0), 0))
    full = lambda a: pl.BlockSpec(a.shape, lambda bi, i: (0, 0))
    return pl.pallas_call(
        _rwkv_prep_body,
        out_shape=[jax.ShapeDtypeStruct((b, s, w), BF16)] * 7
        + [jax.ShapeDtypeStruct((b, s, w), F32)] * 2
        + [jax.ShapeDtypeStruct((b, s // SCAN_TILE, nc, w), F32)],
        grid=(b, s // ts),
        in_specs=[tok(PBM_W), tok(PBL_W), prev(PBM_W), prev(PBL_W)] + [full(p) for p in params],
        out_specs=[tok(w)] * 9 + [pl.BlockSpec((1, nt, nc, w), lambda bi, i: (bi, i, 0, 0))],
        compiler_params=_cparams(("parallel", "parallel"), 48),
        name="rwkv_prep",
    )(pbm, pbl, pbm, pbl, *params)


def _rwkv_scan_body(at_ref, rt_ref, bt_ref, kt_ref, bh_ref, kh_ref, v_ref, g_ref, bonus_ref, ptot_ref,
                    gain_ref, bias_ref, o_ref, state_ref):
    n = B_HEAD
    c_len = CHUNK
    ts = at_ref.shape[1]

    @pl.when(pl.program_id(1) == 0)
    def _():
        state_ref[...] = jnp.zeros_like(state_ref)

    row = lax.broadcasted_iota(jnp.int32, (c_len, 2 * c_len), 0)
    col = lax.broadcasted_iota(jnp.int32, (c_len, 2 * c_len), 1) & (c_len - 1)
    strict2 = col < row
    incl2 = col <= row
    eye = (lax.broadcasted_iota(jnp.int32, (n, n), 0) == lax.broadcasted_iota(jnp.int32, (n, n), 1))
    eye_f = eye.astype(F32)
    zero_blk = jnp.zeros((c_len, n), BF16)
    sq_row = lax.broadcasted_iota(jnp.int32, (c_len, c_len), 0)
    sq_col = lax.broadcasted_iota(jnp.int32, (c_len, c_len), 1)
    same_block = lambda bits: (sq_row >> bits) == (sq_col >> bits)
    diag_blocks = same_block(INV_BASE_BITS)
    merge_masks = [same_block(bits + 1) & ~same_block(bits) for bits in range(INV_BASE_BITS, CHUNK_SHIFT)]

    n_chunks = ts // c_len
    heads = range(at_ref.shape[2] // n)
    units = [(c, h) for c in range(n_chunks) for h in heads]
    rows = lambda c: slice(c * c_len, (c + 1) * c_len)
    lanes = lambda h: slice(h * n, (h + 1) * n)
    load = lambda ref: {(c, h): ref[0, rows(c), lanes(h)] for c, h in units}
    at, rt, bt, kt, bh, kh, vv = (load(r) for r in (at_ref, rt_ref, bt_ref, kt_ref, bh_ref, kh_ref, v_ref))
    state = [state_ref[h] for h in heads]

    g1 = {u: _dot_nt(jnp.concatenate([at[u], rt[u]], axis=0), jnp.concatenate([bt[u], kt[u]], axis=0))
          for u in units}
    a_low = {u: jnp.where(strict2, g1[u][:c_len], 0.0) for u in units}
    r_low = {u: jnp.where(incl2, g1[u][c_len:], 0.0).astype(BF16) for u in units}

    l_mat = {u: a_low[u][:, :c_len] for u in units}
    p = {u: jnp.where(diag_blocks, l_mat[u], 0.0) for u in units}
    t_inv = {u: eye_f + p[u] for u in units}
    p = {u: p[u].astype(BF16) for u in units}
    for _ in range(INV_BASE_BITS - 1):
        p = {u: _dot(p[u], p[u]).astype(BF16) for u in units}
        t_inv = {u: t_inv[u] + _dot(t_inv[u].astype(BF16), p[u]) for u in units}
    for off_mask in merge_masks:
        t_b = {u: t_inv[u].astype(BF16) for u in units}
        t_l = {u: _dot(t_b[u], jnp.where(off_mask, l_mat[u], 0.0).astype(BF16)).astype(BF16) for u in units}
        t_inv = {u: t_inv[u] + _dot(t_l[u], t_b[u]) for u in units}

    akv = {u: _dot(a_low[u][:, c_len:].astype(BF16), vv[u]).astype(BF16) for u in units}
    x = {u: _dot(t_inv[u].astype(BF16), jnp.concatenate([at[u], akv[u]], axis=1)).astype(BF16) for u in units}
    rhs = {u: jnp.concatenate([x[u], jnp.concatenate([zero_blk, vv[u]], axis=1)], axis=0) for u in units}
    qy = {u: _dot(r_low[u], rhs[u]) for u in units}
    wmat = {u: _dot_tn(rhs[u], jnp.concatenate([bh[u], kh[u]], axis=0)) for u in units}
    mc = {(c, h): (jnp.where(eye, ptot_ref[0, 0, c:c + 1, lanes(h)], 0.0) + wmat[(c, h)][:n]).astype(BF16)
          for c, h in units}
    rp = {u: (rt[u].astype(F32) + qy[u][:, :n]).astype(BF16) for u in units}

    y = {}
    for c, h in units:
        s0 = state[h].astype(BF16)
        y[(c, h)] = _dot_nt(rp[(c, h)], s0) + qy[(c, h)][:, n:]
        state[h] = _dot(s0, mc[(c, h)]) + wmat[(c, h)][n:]

    for c in range(n_chunks):
        outs = []
        for h in heads:
            yv = y[(c, h)]
            mu = jnp.mean(yv, axis=-1, keepdims=True)
            yc = yv - mu
            var = jnp.mean(yc * yc, axis=-1, keepdims=True)
            yn = yc * lax.rsqrt(var + HEAD_NORM_EPS) * gain_ref[:, lanes(h)] + bias_ref[:, lanes(h)]
            outs.append((yn + bonus_ref[0, rows(c), lanes(h)]) * g_ref[0, rows(c), lanes(h)])
        o_ref[0, rows(c), :] = jnp.concatenate(outs, axis=1).astype(o_ref.dtype)
    for h in heads:
        state_ref[h] = state[h]


def _rwkv_scan(arrs, gain, bias):
    at = arrs[0]
    b, s, w = at.shape
    ts = SCAN_TILE
    nc = ts // CHUNK
    tok = pl.BlockSpec((1, ts, w), lambda bi, i: (bi, i, 0))
    vec = pl.BlockSpec((1, w), lambda bi, i: (0, 0))
    return pl.pallas_call(
        _rwkv_scan_body,
        out_shape=jax.ShapeDtypeStruct((b, s, w), BF16),
        grid=(b, s // ts),
        in_specs=[tok] * 9 + [pl.BlockSpec((1, 1, nc, w), lambda bi, i: (bi, i, 0, 0)), vec, vec],
        out_specs=tok,
        scratch_shapes=[pltpu.VMEM((w // B_HEAD, B_HEAD, B_HEAD), F32)],
        compiler_params=_cparams(("parallel", "arbitrary"), 32),
        name="rwkv_scan",
    )(*arrs, gain, bias)


def _conv_body(pc_ref, prev_ref, w_ref, g_ref, o_ref):
    w = C_WIDTH
    x = pc_ref[0]
    xp = jnp.where(pl.program_id(1) == 0, 0.0, prev_ref[0])
    u = x[:, w:2 * w] * x[:, 2 * w:3 * w]
    up = xp[:, w:2 * w] * xp[:, 2 * w:3 * w]
    cw = w_ref[...]
    y = cw[0:1] * _shift_rows(up, u, 2) + cw[1:2] * _shift_rows(up, u, 1) + cw[2:3] * u
    z = x[:, 0:w] * y
    ms = _dot_exact_rhs(z * z, _group_ones(w, w // C_GROUPS)) * (C_GROUPS / w)
    o_ref[0] = (z * lax.rsqrt(ms + RMS_EPS) * g_ref[...]).astype(o_ref.dtype)


def _short_conv(pc, conv_w, gain):
    b, s, _ = pc.shape
    ts = PREP_TILE
    return pl.pallas_call(
        _conv_body,
        out_shape=jax.ShapeDtypeStruct((b, s, C_WIDTH), BF16),
        grid=(b, s // ts),
        in_specs=[pl.BlockSpec((1, ts, PC_W), lambda bi, i: (bi, i, 0)),
                  pl.BlockSpec((1, SUBLANES, PC_W),
                               lambda bi, i: (bi, jnp.maximum(i * (ts // SUBLANES) - 1, 0), 0)),
                  pl.BlockSpec(conv_w.shape, lambda bi, i: (0, 0)),
                  pl.BlockSpec(gain.shape, lambda bi, i: (0, 0))],
        out_specs=pl.BlockSpec((1, ts, C_WIDTH), lambda bi, i: (bi, i, 0)),
        compiler_params=_cparams(("parallel", "parallel"), 32),
        name="short_conv",
    )(pc, pc, conv_w, gain)


def _outproj_body(h_ref, ya_ref, yb_ref, yc_ref, w_ref, o_ref):
    acc = h_ref[...] + _dot(ya_ref[...], w_ref[0:A_WIDTH, :])
    acc = acc + _dot(yb_ref[...], w_ref[A_WIDTH:A_WIDTH + B_WIDTH, :])
    o_ref[...] = acc + _dot(yc_ref[...], w_ref[A_WIDTH + B_WIDTH:, :])


def _out_proj(h, ya, yb, yc, w, layer):
    t, d = h.shape
    tm = PROJ_TM
    row = lambda n: pl.BlockSpec((tm, n), lambda i: (i, 0))
    return pl.pallas_call(
        _outproj_body,
        out_shape=jax.ShapeDtypeStruct((t, d), F32),
        grid=(t // tm,),
        in_specs=[row(d), row(A_WIDTH), row(B_WIDTH), row(C_WIDTH),
                  pl.BlockSpec((None,) + w.shape[1:], lambda i: (layer, 0, 0))],
        out_specs=row(d),
        compiler_params=_cparams(("parallel",), 48),
        name="out_proj",
    )(h, ya, yb, yc, w)


def _swap_halves(w):
    half = ROPE_DIM // 2
    return jnp.concatenate([w[..., half:], w[..., :half]], axis=-1)


def _arrange_w_in(w_in):
    w_in = w_in.astype(BF16)
    d = w_in.shape[0]
    o = Q_LORA + KV_LORA
    k_rope = w_in[:, o:A_IN]
    b0 = A_IN
    lora = w_in[:, b0 + PBM_W:b0 + B_IN]
    pad = jnp.zeros((d, PBL_W - lora.shape[1]), w_in.dtype)
    return jnp.concatenate(
        [w_in[:, :Q_LORA], w_in[:, Q_LORA:o], k_rope, _swap_halves(k_rope),
         w_in[:, b0:b0 + PBM_W], lora, pad, w_in[:, b0 + B_IN:]], axis=1)


def _arrange_w_uq(w_uq):
    w = w_uq.astype(BF16).reshape(Q_LORA, A_HEADS, QK_DIM)
    nope = w[:, :, :NOPE_DIM].reshape(Q_LORA, -1)
    rope = w[:, :, NOPE_DIM:]
    return jnp.concatenate(
        [nope, rope.reshape(Q_LORA, -1), _swap_halves(rope).reshape(Q_LORA, -1)], axis=1)


def _arrange_w_ukv(w_ukv):
    w = w_ukv.astype(BF16).reshape(KV_LORA, A_HEADS, NOPE_DIM + V_DIM)
    return jnp.concatenate(
        [w[:, :, :NOPE_DIM].reshape(KV_LORA, -1), w[:, :, NOPE_DIM:].reshape(KV_LORA, -1)], axis=1)


def _pad_rows(w, start):
    full = jnp.zeros((PBL_W, w.shape[1]), F32).at[start:start + w.shape[0]].set(w)
    hi = full.astype(BF16)
    return hi, (full - hi.astype(F32)).astype(BF16)


def _row(v):
    return v.reshape(1, -1)


def kernel(x, positions, norm_ffn1, ffn1_gate, ffn1_up, ffn1_down, norm_mix, w_in, q_norm, kv_norm, w_uq, w_ukv, attn_out_norm, shift_mu, decay_w0, decay_up, iclr_a0, iclr_up, gate_up, k_k, k_a, r_k, lnx_gain, lnx_bias, conv_w, conv_out_norm, w_out, norm_ffn2, ffn2_gate, ffn2_up, ffn2_down, norm_final):
    b, s, d = x.shape
    t = b * s
    seq = lambda a: a.reshape(b, s, a.shape[-1])

    inv_freq = 1.0 / (ROPE_THETA ** (jnp.arange(0, ROPE_DIM, 2, dtype=F32) / ROPE_DIM))
    inv_lane = jnp.tile(inv_freq, LANES // inv_freq.shape[0]).reshape(1, LANES)
    tab = seq(_rope_table(positions.reshape(t, 1), inv_lane))

    h = x.reshape(t, d)
    g_final = _row(norm_final)
    ffn1 = tuple(w.astype(BF16) for w in (ffn1_gate, ffn1_up, ffn1_down))
    ffn2 = tuple(w.astype(BF16) for w in (ffn2_gate, ffn2_up, ffn2_down))
    w_out_b = w_out.astype(BF16)
    for l in range(DEPTH):
        h = _ffn(h, _row(norm_ffn1[l]), *ffn1, l, g_final, False)

        cq, ckv, pbm, pbl, pc = _in_proj(h, _row(norm_mix[l]), _arrange_w_in(w_in[l]))

        q, k, v = _mla_prep(seq(cq), seq(ckv), tab, _row(q_norm[l]), _row(kv_norm[l]),
                            _arrange_w_uq(w_uq[l]), _arrange_w_ukv(w_ukv[l]))
        ya = _attention(q, k, v, _row(attn_out_norm[l]))

        mu = shift_mu[l]
        mu_l = jnp.zeros((PBL_W,), F32).at[:B_IN - PBM_W].set(mu[PBM_W:])
        rw_params = (_row(mu[:PBM_W]), _row(mu_l), _row(decay_w0[l]), _row(iclr_a0[l]),
                     *_pad_rows(decay_up[l], 0), *_pad_rows(iclr_up[l], DECAY_LORA),
                     *_pad_rows(gate_up[l], DECAY_LORA + ICLR_LORA),
                     _row(k_k[l]), _row(k_a[l]), _row(r_k[l]))
        rw = _rwkv_prep(seq(pbm), seq(pbl), rw_params)
        yb = _rwkv_scan(rw, _row(lnx_gain[l]), _row(lnx_bias[l]))

        yc = _short_conv(seq(pc), conv_w[l], _row(conv_out_norm[l]))

        h = _out_proj(h, ya.reshape(t, A_WIDTH), yb.reshape(t, B_WIDTH), yc.reshape(t, C_WIDTH), w_out_b, l)

        h = _ffn(h, _row(norm_ffn2[l]), *ffn2, l, g_final, l == DEPTH - 1)
    return h.reshape(b, s, d)
```

```python
import functools

import jax
import jax.numpy as jnp
from jax import lax
from jax.experimental import pallas as pl
from jax.experimental.pallas import tpu as pltpu

F32 = jnp.float32
BF16 = jnp.bfloat16

D_MODEL = 2048
DEPTH = 4
A_WIDTH = 1024
B_WIDTH = 512
C_WIDTH = 512
V_DIM = 128
NOPE_DIM = 128
ROPE_DIM = 64
QK_DIM = NOPE_DIM + ROPE_DIM
QK_PAD = 256
A_HEADS = 8
Q_LORA = 512
KV_LORA = 256
ROPE_THETA = 10000.0
B_HEAD = 64
B_HEADS = 8
DECAY_LORA = 32
ICLR_LORA = 32
GATE_LORA = 96
HEAD_NORM_EPS = 64e-5
C_GROUPS = 8
CONV_K = 3
D_FF = 5632
RMS_EPS = 1e-6
LOG2_E = 1.4426950408889634
DECAY_LOG_BOUND = 0.6065306597126334
A_IN = Q_LORA + KV_LORA + ROPE_DIM
B_IN = 3 * B_WIDTH + DECAY_LORA + ICLR_LORA + GATE_LORA

CKV_W = KV_LORA + 2 * ROPE_DIM
PBM_W = 3 * B_WIDTH
PBL_W = 256
PC_W = 3 * C_WIDTH
P_SPLITS = (Q_LORA, CKV_W, PBM_W, PBL_W, PC_W)
P_TOTAL = sum(P_SPLITS)

V7X_VMEM_BYTES = 64 * 1024 * 1024
SUBLANES = 8
LANES = 128
FFN_TM = 1024
FFN_TF = 512
PROJ_TM = 512
SCAN_TILE = 256
PREP_TILE = 512
CUM_TILE = 256
ATTN_TQ = 256
CHUNK = 64
CHUNK_SHIFT = CHUNK.bit_length() - 1
INV_BASE_BITS = 3
ROPE_TW = 3 * LANES


def _cparams(sem, vmem_mb):
    return pltpu.CompilerParams(dimension_semantics=sem, vmem_limit_bytes=vmem_mb * 1024 * 1024)


def _rms(x, g, eps=RMS_EPS):
    return x * lax.rsqrt(jnp.mean(x * x, axis=-1, keepdims=True) + eps) * g


def _dot(a, b):
    return jnp.dot(a, b, preferred_element_type=F32)


def _dot_nt(a, b):
    return lax.dot_general(a, b, (((1,), (1,)), ((), ())), preferred_element_type=F32)


def _dot_tn(a, b):
    return lax.dot_general(a, b, (((0,), (0,)), ((), ())), preferred_element_type=F32)


def _split(x):
    hi = x.astype(BF16)
    lo = (x - hi.astype(F32)).astype(BF16)
    return hi, lo


def _dot_exact_rhs(x, m2):
    hi, lo = _split(x)
    return _dot(jnp.concatenate([hi, lo], axis=1), m2)


def _dot3(x, w3):
    hi, lo = _split(x)
    return _dot(jnp.concatenate([hi, lo, hi], axis=1), w3)


def _shift_rows(prev8, x, k):
    cat = jnp.concatenate([prev8, x], axis=0)
    return pltpu.roll(cat, k, axis=0)[SUBLANES:]


def _group_ones2(n, group):
    shift = group.bit_length() - 1
    r = (lax.broadcasted_iota(jnp.int32, (2 * n, n), 0) & (n - 1)) >> shift
    c = lax.broadcasted_iota(jnp.int32, (2 * n, n), 1) >> shift
    return (r == c).astype(BF16)


def _rope_body(pos_ref, inv_ref, o_ref):
    ang = pos_ref[...].astype(F32) * inv_ref[...]
    c = jnp.cos(ang)
    s = jnp.sin(ang)
    lane = lax.broadcasted_iota(jnp.int32, ang.shape, 1)
    ssg = jnp.where((lane & (ROPE_DIM - 1)) < ROPE_DIM // 2, -s, s)
    o_ref[:, 0:LANES] = c
    o_ref[:, LANES:2 * LANES] = ssg
    o_ref[:, 2 * LANES:3 * LANES] = jnp.where(lane < ROPE_DIM, c, ssg)


def _rope_table(pos_col, inv_lane):
    t = pos_col.shape[0]
    tm = 1024
    return pl.pallas_call(
        _rope_body,
        out_shape=jax.ShapeDtypeStruct((t, ROPE_TW), F32),
        grid=(t // tm,),
        in_specs=[pl.BlockSpec((tm, 1), lambda i: (i, 0)),
                  pl.BlockSpec((1, LANES), lambda i: (0, 0))],
        out_specs=pl.BlockSpec((tm, ROPE_TW), lambda i: (i, 0)),
        compiler_params=_cparams(("parallel",), 16),
        name="rope_table",
    )(pos_col, inv_lane)


def _ffn_body(x_ref, g_ref, wg_ref, wu_ref, wd_ref, gf_ref, o_ref, xn_ref, *, final):
    j = pl.program_id(1)

    @pl.when(j == 0)
    def _():
        x = x_ref[...]
        xn_ref[...] = _rms(x, g_ref[...]).astype(BF16)
        o_ref[...] = x

    xn = xn_ref[...]
    gate = _dot(xn, wg_ref[...])
    up = _dot(xn, wu_ref[...])
    act = (gate * jax.nn.sigmoid(gate)) * (up * 0.5)
    o_ref[...] += _dot(act.astype(BF16), wd_ref[...])

    if final:
        @pl.when(j == pl.num_programs(1) - 1)
        def _():
            o_ref[...] = _rms(o_ref[...], gf_ref[...])


def _ffn(h, g, wg, wu, wd, layer, g_final, final):
    t, d = h.shape
    f = wg.shape[2]
    tm, tf = FFN_TM, FFN_TF
    return pl.pallas_call(
        functools.partial(_ffn_body, final=final),
        out_shape=jax.ShapeDtypeStruct((t, d), F32),
        grid=(t // tm, f // tf),
        in_specs=[pl.BlockSpec((tm, d), lambda i, j: (i, 0)),
                  pl.BlockSpec((1, d), lambda i, j: (0, 0)),
                  pl.BlockSpec((None, d, tf), lambda i, j: (layer, 0, j)),
                  pl.BlockSpec((None, d, tf), lambda i, j: (layer, 0, j)),
                  pl.BlockSpec((None, tf, d), lambda i, j: (layer, j, 0)),
                  pl.BlockSpec((1, d), lambda i, j: (0, 0))],
        out_specs=pl.BlockSpec((tm, d), lambda i, j: (i, 0)),
        scratch_shapes=[pltpu.VMEM((tm, d), BF16)],
        compiler_params=_cparams(("parallel", "arbitrary"), 60),
        name="ffn",
    )(h, g, wg, wu, wd, g_final)


def _inproj_body(x_ref, g_ref, w_ref, cq_ref, ckv_ref, pbm_ref, pbl_ref, pc_ref):
    xn = _rms(x_ref[...], g_ref[...]).astype(BF16)
    off = 0
    for ref in (cq_ref, ckv_ref, pbm_ref, pbl_ref, pc_ref):
        n = ref.shape[-1]
        ref[...] = _dot(xn, w_ref[:, off:off + n])
        off += n


def _in_proj(h, g, w):
    t, d = h.shape
    tm = PROJ_TM
    return pl.pallas_call(
        _inproj_body,
        out_shape=[jax.ShapeDtypeStruct((t, n), F32) for n in P_SPLITS],
        grid=(t // tm,),
        in_specs=[pl.BlockSpec((tm, d), lambda i: (i, 0)),
                  pl.BlockSpec((1, d), lambda i: (0, 0)),
                  pl.BlockSpec((d, P_TOTAL), lambda i: (0, 0), pipeline_mode=pl.Buffered(1))],
        out_specs=[pl.BlockSpec((tm, n), lambda i: (i, 0)) for n in P_SPLITS],
        compiler_params=_cparams(("parallel",), 56),
        name="in_proj",
    )(h, g, w)


def _mla_prep_body(cq_ref, ckv_ref, tab_ref, qn_ref, kvn_ref, wq_ref, wkv_ref, q_ref, k_ref, v_ref):
    nq = A_HEADS * NOPE_DIM
    nr = A_HEADS * ROPE_DIM
    scale = QK_DIM ** -0.5 * LOG2_E
    tab = tab_ref[0]
    cos2 = tab[:, 0:LANES]
    sin2 = tab[:, LANES:2 * LANES]
    cs = tab[:, 2 * LANES:3 * LANES]

    cqn = _rms(cq_ref[0], qn_ref[...]).astype(BF16)
    qa = _dot(cqn, wq_ref[...])
    cos_h = jnp.concatenate([cos2] * (nr // LANES), axis=1)
    sin_h = jnp.concatenate([sin2] * (nr // LANES), axis=1)
    q_rot = qa[:, nq:nq + nr] * cos_h + qa[:, nq + nr:nq + 2 * nr] * sin_h

    ckv = ckv_ref[0]
    ckvn = _rms(ckv[:, :KV_LORA], kvn_ref[...]).astype(BF16)
    kv = _dot(ckvn, wkv_ref[...])
    t = ckv[:, KV_LORA:KV_LORA + LANES] * cs
    low_half = lax.broadcasted_iota(jnp.int32, t.shape, 1) < ROPE_DIM
    k_rot = jnp.where(low_half, t + pltpu.roll(t, ROPE_DIM, axis=1), 0.0).astype(BF16)
    for h in range(A_HEADS):
        pair = q_rot[:, (h // 2) * LANES:(h // 2 + 1) * LANES]
        if h % 2:
            pair = pltpu.roll(pair, ROPE_DIM, axis=1)
        q_rope = jnp.where(low_half, pair, 0.0)
        q_nope = qa[:, h * NOPE_DIM:(h + 1) * NOPE_DIM]
        q_ref[0, h] = (jnp.concatenate([q_nope, q_rope], axis=1) * scale).astype(BF16)
        k_ref[0, h] = jnp.concatenate([kv[:, h * NOPE_DIM:(h + 1) * NOPE_DIM].astype(BF16), k_rot], axis=1)
        v_ref[0, h] = kv[:, nq + h * V_DIM:nq + (h + 1) * V_DIM].astype(BF16)


def _mla_prep(cq, ckv, tab, qn, kvn, wq, wkv):
    b, s, _ = cq.shape
    ts = PREP_TILE
    tok = lambda n: pl.BlockSpec((1, ts, n), lambda bi, i: (bi, i, 0))
    full = lambda a: pl.BlockSpec(a.shape, lambda bi, i: (0, 0))
    head = lambda n: pl.BlockSpec((1, A_HEADS, ts, n), lambda bi, i: (bi, 0, i, 0))
    return pl.pallas_call(
        _mla_prep_body,
        out_shape=[jax.ShapeDtypeStruct((b, A_HEADS, s, QK_PAD), BF16),
                   jax.ShapeDtypeStruct((b, A_HEADS, s, QK_PAD), BF16),
                   jax.ShapeDtypeStruct((b, A_HEADS, s, V_DIM), BF16)],
        grid=(b, s // ts),
        in_specs=[tok(Q_LORA), tok(CKV_W), tok(ROPE_TW), full(qn), full(kvn), full(wq), full(wkv)],
        out_specs=[head(QK_PAD), head(QK_PAD), head(V_DIM)],
        compiler_params=_cparams(("parallel", "parallel"), 48),
        name="mla_prep",
    )(cq, ckv, tab, qn, kvn, wq, wkv)


def _attn_body(q_ref, k_ref, v_ref, g_ref, o_ref, s_ref, mx_ref, m_ref, acc_ref):
    i = pl.program_id(1)
    nh = q_ref.shape[1]
    tq = q_ref.shape[2]
    heads = range(nh)
    ones = jnp.ones((tq, V_DIM), BF16)
    keys = lambda j: pl.ds(pl.multiple_of(j * tq, tq), tq)

    row = lax.broadcasted_iota(jnp.int32, (tq, tq), 0)
    col = lax.broadcasted_iota(jnp.int32, (tq, tq), 1)
    for h in heads:
        s = jnp.where(col <= row, _dot_nt(q_ref[0, h], k_ref[0, h, keys(i), :]), -1e30)
        s_ref[h, i] = s
        mx_ref[h] = s

    def scores(j, _):
        for h in heads:
            s = _dot_nt(q_ref[0, h], k_ref[0, h, keys(j), :])
            s_ref[h, j] = s
            mx_ref[h] = jnp.maximum(mx_ref[h], s)
        return 0

    lax.fori_loop(0, i, scores, 0)
    for h in heads:
        m_ref[h] = jnp.broadcast_to(jnp.max(mx_ref[h], axis=-1, keepdims=True), (tq, LANES))

    def weighted(j, h):
        m = m_ref[h]
        p = jnp.exp2(s_ref[h, j] - jnp.concatenate([m] * (tq // LANES), axis=1))
        return _dot(p.astype(BF16), jnp.concatenate([v_ref[0, h, keys(j), :], ones], axis=1))

    for h in heads:
        acc_ref[h] = weighted(i, h)

    def values(j, _):
        for h in heads:
            acc_ref[h] += weighted(j, h)
        return 0

    lax.fori_loop(0, i, values, 0)
    for h in heads:
        acc = acc_ref[h]
        hs = slice(h * V_DIM, (h + 1) * V_DIM)
        o_ref[0, :, hs] = _rms(acc[:, :V_DIM] / acc[:, V_DIM:], g_ref[:, hs]).astype(o_ref.dtype)


def _attention(q, k, v, g):
    b, nh, s, _ = q.shape
    tq = ATTN_TQ
    return pl.pallas_call(
        _attn_body,
        out_shape=jax.ShapeDtypeStruct((b, s, nh * V_DIM), BF16),
        grid=(b, s // tq),
        in_specs=[pl.BlockSpec((1, nh, tq, QK_PAD), lambda bi, i: (bi, 0, i, 0)),
                  pl.BlockSpec((1, nh, s, QK_PAD), lambda bi, i: (bi, 0, 0, 0)),
                  pl.BlockSpec((1, nh, s, V_DIM), lambda bi, i: (bi, 0, 0, 0)),
                  pl.BlockSpec((1, nh * V_DIM), lambda bi, i: (0, 0))],
        out_specs=pl.BlockSpec((1, tq, nh * V_DIM), lambda bi, i: (bi, i, 0)),
        scratch_shapes=[pltpu.VMEM((nh, s // tq, tq, tq), F32),
                        pltpu.VMEM((nh, tq, tq), F32),
                        pltpu.VMEM((nh, tq, LANES), F32),
                        pltpu.VMEM((nh, tq, 2 * V_DIM), F32)],
        compiler_params=_cparams(("parallel", "arbitrary"), 56),
        name="mla_attention",
    )(q, k, v, g)


def _rwkv_prep_body(pbm_ref, pbl_ref, pbm_prev_ref, pbl_prev_ref, mum_ref, mul_ref, w0_ref, a0_ref,
                    dec_ref, icl_ref, gat_ref, kk_ref, ka_ref, rk_ref,
                    at_ref, rt_ref, bt_ref, kt_ref, bh_ref, kh_ref, v_ref, g_ref, bonus_ref, ptot_ref):
    i = pl.program_id(1)
    ts = pbm_ref.shape[1]
    w = B_WIDTH
    first = i == 0

    xm = pbm_ref[0]
    xl = pbl_ref[0]
    pm = jnp.where(first, 0.0, pbm_prev_ref[0])
    pv = jnp.where(first, 0.0, pbl_prev_ref[0])
    xs = xm + (_shift_rows(pm, xm, 1) - xm) * mum_ref[...]
    ls = xl + (_shift_rows(pv, xl, 1) - xl) * mul_ref[...]
    r = xs[:, 0:w]
    k = xs[:, w:2 * w]
    v = xs[:, 2 * w:3 * w]

    wl = w0_ref[...] + _dot3(jnp.tanh(ls), dec_ref[...])
    lw = (-DECAY_LOG_BOUND * LOG2_E) * jax.nn.sigmoid(wl)
    a_ic = jax.nn.sigmoid(a0_ref[...] + _dot3(ls, icl_ref[...]))
    gate = _dot3(jax.nn.sigmoid(ls), gat_ref[...])

    seg2 = _group_ones2(w, B_HEAD)
    kk = k * kk_ref[...]
    kkn = kk * jnp.minimum(lax.rsqrt(_dot_exact_rhs(kk * kk, seg2)), 1e12)
    kp = k * (1.0 + (a_ic - 1.0) * ka_ref[...])
    bonus = _dot_exact_rhs(r * kp * rk_ref[...], seg2) * v

    ct = CUM_TILE
    row = lax.broadcasted_iota(jnp.int32, (2 * ct, 2 * ct), 0)
    col = lax.broadcasted_iota(jnp.int32, (2 * ct, 2 * ct), 1) & (ct - 1)
    same = ((row & (ct - 1)) >> CHUNK_SHIFT) == (col >> CHUNK_SHIFT)
    pos = row & (ct - 1)
    first_col = jnp.where(row < ct, 0, pos + 1)
    last_col = jnp.where(row < ct, pos, ct)
    tri2 = (same & (col >= first_col) & (col <= last_col)).astype(BF16)
    lw_hi, lw_lo = _split(lw)
    cums = [_dot(tri2, jnp.concatenate([lw_hi[t0:t0 + ct], lw_lo[t0:t0 + ct]], axis=0))
            for t0 in range(0, ts, ct)]
    cum = jnp.concatenate([c[:ct] for c in cums], axis=0)
    cumr = jnp.concatenate([c[ct:] for c in cums], axis=0)

    e_neg = jnp.exp2(-cum)
    e_rem = jnp.exp2(cumr)
    b_vec = kkn * a_ic
    at_ref[0] = (-kkn * jnp.exp2(cum - lw)).astype(BF16)
    rt_ref[0] = (r * jnp.exp2(cum)).astype(BF16)
    bt_ref[0] = (b_vec * e_neg).astype(BF16)
    kt_ref[0] = (kp * e_neg).astype(BF16)
    bh_ref[0] = (b_vec * e_rem).astype(BF16)
    kh_ref[0] = (kp * e_rem).astype(BF16)
    v_ref[0] = v.astype(BF16)
    g_ref[0] = gate
    bonus_ref[0] = bonus
    nc = ptot_ref.shape[2]
    for t in range(ptot_ref.shape[1]):
        last_rows = [(t * nc + c + 1) * CHUNK - 1 for c in range(nc)]
        ptot_ref[0, t] = jnp.exp2(jnp.concatenate([cum[r0:r0 + 1] for r0 in last_rows], axis=0))


def _rwkv_prep(pbm, pbl, params):
    b, s, _ = pbm.shape
    ts = PREP_TILE
    nc = SCAN_TILE // CHUNK
    nt = ts // SCAN_TILE
    w = B_WIDTH
    tok = lambda n: pl.BlockSpec((1, ts, n), lambda bi, i: (bi, i, 0))
    prev = lambda n: pl.BlockSpec(
        (1, SUBLANES, n), lambda bi, i: (bi, jnp.maximum(i * (ts // SUBLANES) - 1, 0), 0))
    full = lambda a: pl.BlockSpec(a.shape, lambda bi, i: (0, 0))
    return pl.pallas_call(
        _rwkv_prep_body,
        out_shape=[jax.ShapeDtypeStruct((b, s, w), BF16)] * 7
        + [jax.ShapeDtypeStruct((b, s, w), F32)] * 2
        + [jax.ShapeDtypeStruct((b, s // SCAN_TILE, nc, w), F32)],
        grid=(b, s // ts),
        in_specs=[tok(PBM_W), tok(PBL_W), prev(PBM_W), prev(PBL_W)] + [full(p) for p in params],
        out_specs=[tok(w)] * 9 + [pl.BlockSpec((1, nt, nc, w), lambda bi, i: (bi, i, 0, 0))],
        compiler_params=_cparams(("parallel", "parallel"), 48),
        name="rwkv_prep",
    )(pbm, pbl, pbm, pbl, *params)


def _rwkv_scan_body(at_ref, rt_ref, bt_ref, kt_ref, bh_ref, kh_ref, v_ref, g_ref, bonus_ref, ptot_ref,
                    gain_ref, bias_ref, o_ref, state_ref):
    n = B_HEAD
    c_len = CHUNK
    ts = at_ref.shape[1]

    @pl.when(pl.program_id(1) == 0)
    def _():
        state_ref[...] = jnp.zeros_like(state_ref)

    row = lax.broadcasted_iota(jnp.int32, (c_len, 2 * c_len), 0)
    col = lax.broadcasted_iota(jnp.int32, (c_len, 2 * c_len), 1) & (c_len - 1)
    strict2 = col < row
    incl2 = col <= row
    eye = (lax.broadcasted_iota(jnp.int32, (n, n), 0) == lax.broadcasted_iota(jnp.int32, (n, n), 1))
    eye_f = eye.astype(F32)
    zero_blk = jnp.zeros((c_len, n), BF16)
    sq_row = lax.broadcasted_iota(jnp.int32, (c_len, c_len), 0)
    sq_col = lax.broadcasted_iota(jnp.int32, (c_len, c_len), 1)
    same_block = lambda bits: (sq_row >> bits) == (sq_col >> bits)
    diag_blocks = same_block(INV_BASE_BITS)
    merge_masks = [same_block(bits + 1) & ~same_block(bits) for bits in range(INV_BASE_BITS, CHUNK_SHIFT)]

    n_chunks = ts // c_len
    heads = range(at_ref.shape[2] // n)
    units = [(c, h) for c in range(n_chunks) for h in heads]
    rows = lambda c: slice(c * c_len, (c + 1) * c_len)
    lanes = lambda h: slice(h * n, (h + 1) * n)
    load = lambda ref: {(c, h): ref[0, rows(c), lanes(h)] for c, h in units}
    at, rt, bt, kt, bh, kh, vv = (load(r) for r in (at_ref, rt_ref, bt_ref, kt_ref, bh_ref, kh_ref, v_ref))
    state = [state_ref[h] for h in heads]

    g1 = {u: _dot_nt(jnp.concatenate([at[u], rt[u]], axis=0), jnp.concatenate([bt[u], kt[u]], axis=0))
          for u in units}
    a_low = {u: jnp.where(strict2, g1[u][:c_len], 0.0) for u in units}
    r_low = {u: jnp.where(incl2, g1[u][c_len:], 0.0).astype(BF16) for u in units}

    l_mat = {u: a_low[u][:, :c_len] for u in units}
    p = {u: jnp.where(diag_blocks, l_mat[u], 0.0) for u in units}
    t_inv = {u: eye_f + p[u] for u in units}
    p = {u: p[u].astype(BF16) for u in units}
    for _ in range(INV_BASE_BITS - 1):
        p = {u: _dot(p[u], p[u]).astype(BF16) for u in units}
        t_inv = {u: t_inv[u] + _dot(t_inv[u].astype(BF16), p[u]) for u in units}
    for off_mask in merge_masks:
        t_b = {u: t_inv[u].astype(BF16) for u in units}
        t_l = {u: _dot(t_b[u], jnp.where(off_mask, l_mat[u], 0.0).astype(BF16)).astype(BF16) for u in units}
        t_inv = {u: t_inv[u] + _dot(t_l[u], t_b[u]) for u in units}

    akv = {u: _dot(a_low[u][:, c_len:].astype(BF16), vv[u]).astype(BF16) for u in units}
    x = {u: _dot(t_inv[u].astype(BF16), jnp.concatenate([at[u], akv[u]], axis=1)).astype(BF16) for u in units}
    rhs = {u: jnp.concatenate([x[u], jnp.concatenate([zero_blk, vv[u]], axis=1)], axis=0) for u in units}
    qy = {u: _dot(r_low[u], rhs[u]) for u in units}
    wmat = {u: _dot_tn(rhs[u], jnp.concatenate([bh[u], kh[u]], axis=0)) for u in units}
    mc = {(c, h): (jnp.where(eye, ptot_ref[0, 0, c:c + 1, lanes(h)], 0.0) + wmat[(c, h)][:n]).astype(BF16)
          for c, h in units}
    rp = {u: (rt[u].astype(F32) + qy[u][:, :n]).astype(BF16) for u in units}

    y = {}
    for c, h in units:
        s0 = state[h].astype(BF16)
        y[(c, h)] = _dot_nt(rp[(c, h)], s0) + qy[(c, h)][:, n:]
        state[h] = _dot(s0, mc[(c, h)]) + wmat[(c, h)][n:]

    for c in range(n_chunks):
        outs = []
        for h in heads:
            yv = y[(c, h)]
            mu = jnp.mean(yv, axis=-1, keepdims=True)
            yc = yv - mu
            var = jnp.mean(yc * yc, axis=-1, keepdims=True)
            yn = yc * lax.rsqrt(var + HEAD_NORM_EPS) * gain_ref[:, lanes(h)] + bias_ref[:, lanes(h)]
            outs.append((yn + bonus_ref[0, rows(c), lanes(h)]) * g_ref[0, rows(c), lanes(h)])
        o_ref[0, rows(c), :] = jnp.concatenate(outs, axis=1).astype(o_ref.dtype)
    for h in heads:
        state_ref[h] = state[h]


def _rwkv_scan(arrs, gain, bias):
    at = arrs[0]
    b, s, w = at.shape
    ts = SCAN_TILE
    nc = ts // CHUNK
    tok = pl.BlockSpec((1, ts, w), lambda bi, i: (bi, i, 0))
    vec = pl.BlockSpec((1, w), lambda bi, i: (0, 0))
    return pl.pallas_call(
        _rwkv_scan_body,
        out_shape=jax.ShapeDtypeStruct((b, s, w), BF16),
        grid=(b, s // ts),
        in_specs=[tok] * 9 + [pl.BlockSpec((1, 1, nc, w), lambda bi, i: (bi, i, 0, 0)), vec, vec],
        out_specs=tok,
        scratch_shapes=[pltpu.VMEM((w // B_HEAD, B_HEAD, B_HEAD), F32)],
        compiler_params=_cparams(("parallel", "arbitrary"), 32),
        name="rwkv_scan",
    )(*arrs, gain, bias)


def _conv_body(pc_ref, prev_ref, w_ref, g_ref, o_ref):
    w = C_WIDTH
    x = pc_ref[0]
    xp = jnp.where(pl.program_id(1) == 0, 0.0, prev_ref[0])
    u = x[:, w:2 * w] * x[:, 2 * w:3 * w]
    up = xp[:, w:2 * w] * xp[:, 2 * w:3 * w]
    cw = w_ref[...]
    y = cw[0:1] * _shift_rows(up, u, 2) + cw[1:2] * _shift_rows(up, u, 1) + cw[2:3] * u
    z = x[:, 0:w] * y
    ms = _dot_exact_rhs(z * z, _group_ones2(w, w // C_GROUPS)) * (C_GROUPS / w)
    o_ref[0] = (z * lax.rsqrt(ms + RMS_EPS) * g_ref[...]).astype(o_ref.dtype)


def _short_conv(pc, conv_w, gain):
    b, s, _ = pc.shape
    ts = PREP_TILE
    return pl.pallas_call(
        _conv_body,
        out_shape=jax.ShapeDtypeStruct((b, s, C_WIDTH), BF16),
        grid=(b, s // ts),
        in_specs=[pl.BlockSpec((1, ts, PC_W), lambda bi, i: (bi, i, 0)),
                  pl.BlockSpec((1, SUBLANES, PC_W),
                               lambda bi, i: (bi, jnp.maximum(i * (ts // SUBLANES) - 1, 0), 0)),
                  pl.BlockSpec(conv_w.shape, lambda bi, i: (0, 0)),
                  pl.BlockSpec(gain.shape, lambda bi, i: (0, 0))],
        out_specs=pl.BlockSpec((1, ts, C_WIDTH), lambda bi, i: (bi, i, 0)),
        compiler_params=_cparams(("parallel", "parallel"), 32),
        name="short_conv",
    )(pc, pc, conv_w, gain)


def _outproj_body(h_ref, ya_ref, yb_ref, yc_ref, w_ref, o_ref):
    acc = h_ref[...] + _dot(ya_ref[...], w_ref[0:A_WIDTH, :])
    acc = acc + _dot(yb_ref[...], w_ref[A_WIDTH:A_WIDTH + B_WIDTH, :])
    o_ref[...] = acc + _dot(yc_ref[...], w_ref[A_WIDTH + B_WIDTH:, :])


def _out_proj(h, ya, yb, yc, w, layer):
    t, d = h.shape
    tm = PROJ_TM
    row = lambda n: pl.BlockSpec((tm, n), lambda i: (i, 0))
    return pl.pallas_call(
        _outproj_body,
        out_shape=jax.ShapeDtypeStruct((t, d), F32),
        grid=(t // tm,),
        in_specs=[row(d), row(A_WIDTH), row(B_WIDTH), row(C_WIDTH),
                  pl.BlockSpec((None,) + w.shape[1:], lambda i: (layer, 0, 0))],
        out_specs=row(d),
        compiler_params=_cparams(("parallel",), 48),
        name="out_proj",
    )(h, ya, yb, yc, w)


def _swap_halves(w):
    half = ROPE_DIM // 2
    return jnp.concatenate([w[..., half:], w[..., :half]], axis=-1)


def _arrange_w_in(w_in):
    w_in = w_in.astype(BF16)
    d = w_in.shape[0]
    o = Q_LORA + KV_LORA
    k_rope = w_in[:, o:A_IN]
    b0 = A_IN
    lora = w_in[:, b0 + PBM_W:b0 + B_IN]
    pad = jnp.zeros((d, PBL_W - lora.shape[1]), w_in.dtype)
    return jnp.concatenate(
        [w_in[:, :Q_LORA], w_in[:, Q_LORA:o], k_rope, _swap_halves(k_rope),
         w_in[:, b0:b0 + PBM_W], lora, pad, w_in[:, b0 + B_IN:]], axis=1)


def _arrange_w_uq(w_uq):
    w = w_uq.astype(BF16).reshape(Q_LORA, A_HEADS, QK_DIM)
    nope = w[:, :, :NOPE_DIM].reshape(Q_LORA, -1)
    rope = w[:, :, NOPE_DIM:]
    return jnp.concatenate(
        [nope, rope.reshape(Q_LORA, -1), _swap_halves(rope).reshape(Q_LORA, -1)], axis=1)


def _arrange_w_ukv(w_ukv):
    w = w_ukv.astype(BF16).reshape(KV_LORA, A_HEADS, NOPE_DIM + V_DIM)
    return jnp.concatenate(
        [w[:, :, :NOPE_DIM].reshape(KV_LORA, -1), w[:, :, NOPE_DIM:].reshape(KV_LORA, -1)], axis=1)


def _pad_rows(w, start):
    full = jnp.zeros((PBL_W, w.shape[1]), F32).at[start:start + w.shape[0]].set(w)
    hi = full.astype(BF16)
    return jnp.concatenate([hi, hi, (full - hi.astype(F32)).astype(BF16)], axis=0)


def _row(v):
    return v.reshape(1, -1)


def kernel(x, positions, norm_ffn1, ffn1_gate, ffn1_up, ffn1_down, norm_mix, w_in, q_norm, kv_norm, w_uq, w_ukv, attn_out_norm, shift_mu, decay_w0, decay_up, iclr_a0, iclr_up, gate_up, k_k, k_a, r_k, lnx_gain, lnx_bias, conv_w, conv_out_norm, w_out, norm_ffn2, ffn2_gate, ffn2_up, ffn2_down, norm_final):
    b, s, d = x.shape
    t = b * s
    seq = lambda a: a.reshape(b, s, a.shape[-1])

    inv_freq = 1.0 / (ROPE_THETA ** (jnp.arange(0, ROPE_DIM, 2, dtype=F32) / ROPE_DIM))
    inv_lane = jnp.tile(inv_freq, LANES // inv_freq.shape[0]).reshape(1, LANES)
    tab = seq(_rope_table(positions.reshape(t, 1), inv_lane))

    h = x.reshape(t, d)
    g_final = _row(norm_final)
    ffn1 = tuple(w.astype(BF16) for w in (ffn1_gate, ffn1_up, ffn1_down))
    ffn2 = tuple(w.astype(BF16) for w in (ffn2_gate, ffn2_up, ffn2_down))
    w_out_b = w_out.astype(BF16)
    for l in range(DEPTH):
        h = _ffn(h, _row(norm_ffn1[l]), *ffn1, l, g_final, False)

        cq, ckv, pbm, pbl, pc = _in_proj(h, _row(norm_mix[l]), _arrange_w_in(w_in[l]))

        q, k, v = _mla_prep(seq(cq), seq(ckv), tab, _row(q_norm[l]), _row(kv_norm[l]),
                            _arrange_w_uq(w_uq[l]), _arrange_w_ukv(w_ukv[l]))
        ya = _attention(q, k, v, _row(attn_out_norm[l]))

        mu = shift_mu[l]
        mu_l = jnp.zeros((PBL_W,), F32).at[:B_IN - PBM_W].set(mu[PBM_W:])
        rw_params = (_row(mu[:PBM_W]), _row(mu_l), _row(decay_w0[l]), _row(iclr_a0[l]),
                     _pad_rows(decay_up[l], 0), _pad_rows(iclr_up[l], DECAY_LORA),
                     _pad_rows(gate_up[l], DECAY_LORA + ICLR_LORA),
                     _row(k_k[l]), _row(k_a[l]), _row(r_k[l]))
        rw = _rwkv_prep(seq(pbm), seq(pbl), rw_params)
        yb = _rwkv_scan(rw, _row(lnx_gain[l]), _row(lnx_bias[l]))

        yc = _short_conv(seq(pc), conv_w[l], _row(conv_out_norm[l]))

        h = _out_proj(h, ya.reshape(t, A_WIDTH), yb.reshape(t, B_WIDTH), yc.reshape(t, C_WIDTH), w_out_b, l)

        h = _ffn(h, _row(norm_ffn2[l]), *ffn2, l, g_final, l == DEPTH - 1)
    return h.reshape(b, s, d)
```

```python
import functools

import jax
import jax.numpy as jnp
from jax import lax
from jax.experimental import pallas as pl
from jax.experimental.pallas import tpu as pltpu

F32 = jnp.float32
BF16 = jnp.bfloat16

D_MODEL = 2048
DEPTH = 4
A_WIDTH = 1024
B_WIDTH = 512
C_WIDTH = 512
V_DIM = 128
NOPE_DIM = 128
ROPE_DIM = 64
QK_DIM = NOPE_DIM + ROPE_DIM
QK_PAD = 256
A_HEADS = 8
Q_LORA = 512
KV_LORA = 256
ROPE_THETA = 10000.0
B_HEAD = 64
B_HEADS = 8
DECAY_LORA = 32
ICLR_LORA = 32
GATE_LORA = 96
HEAD_NORM_EPS = 64e-5
C_GROUPS = 8
CONV_K = 3
D_FF = 5632
RMS_EPS = 1e-6
LOG2_E = 1.4426950408889634
DECAY_LOG_BOUND = 0.6065306597126334
A_IN = Q_LORA + KV_LORA + ROPE_DIM
B_IN = 3 * B_WIDTH + DECAY_LORA + ICLR_LORA + GATE_LORA

CKV_W = KV_LORA + 2 * ROPE_DIM
PBM_W = 3 * B_WIDTH
PBL_W = 256
PC_W = 3 * C_WIDTH
P_SPLITS = (Q_LORA, CKV_W, PBM_W, PBL_W, PC_W)
P_TOTAL = sum(P_SPLITS)

V7X_VMEM_BYTES = 64 * 1024 * 1024
SUBLANES = 8
LANES = 128
FFN_TM = 1024
FFN_TF = 512
PROJ_TM = 512
SCAN_TILE = 256
PREP_TILE = 512
CUM_TILE = 256
ATTN_TQ = 256
CHUNK = 64
CHUNK_SHIFT = CHUNK.bit_length() - 1
INV_BASE_BITS = 3
ROPE_TW = 3 * LANES


def _cparams(sem, vmem_mb):
    return pltpu.CompilerParams(dimension_semantics=sem, vmem_limit_bytes=vmem_mb * 1024 * 1024)


def _rms(x, g, eps=RMS_EPS):
    return x * lax.rsqrt(jnp.mean(x * x, axis=-1, keepdims=True) + eps) * g


def _dot(a, b):
    return jnp.dot(a, b, preferred_element_type=F32)


def _dot_nt(a, b):
    return lax.dot_general(a, b, (((1,), (1,)), ((), ())), preferred_element_type=F32)


def _dot_tn(a, b):
    return lax.dot_general(a, b, (((0,), (0,)), ((), ())), preferred_element_type=F32)


def _split(x):
    hi = x.astype(BF16)
    lo = (x - hi.astype(F32)).astype(BF16)
    return hi, lo


def _dot_exact_rhs(x, m2):
    hi, lo = _split(x)
    return _dot(jnp.concatenate([hi, lo], axis=1), m2)


def _dot3(x, w3):
    hi, lo = _split(x)
    return _dot(jnp.concatenate([hi, lo, hi], axis=1), w3)


def _shift_rows(prev8, x, k):
    cat = jnp.concatenate([prev8, x], axis=0)
    return pltpu.roll(cat, k, axis=0)[SUBLANES:]


def _group_ones2(n, group):
    shift = group.bit_length() - 1
    r = (lax.broadcasted_iota(jnp.int32, (2 * n, n), 0) & (n - 1)) >> shift
    c = lax.broadcasted_iota(jnp.int32, (2 * n, n), 1) >> shift
    return (r == c).astype(BF16)


def _rope_body(pos_ref, inv_ref, o_ref):
    ang = pos_ref[...].astype(F32) * inv_ref[...]
    c = jnp.cos(ang)
    s = jnp.sin(ang)
    lane = lax.broadcasted_iota(jnp.int32, ang.shape, 1)
    ssg = jnp.where((lane & (ROPE_DIM - 1)) < ROPE_DIM // 2, -s, s)
    o_ref[:, 0:LANES] = c
    o_ref[:, LANES:2 * LANES] = ssg
    o_ref[:, 2 * LANES:3 * LANES] = jnp.where(lane < ROPE_DIM, c, ssg)


def _rope_table(pos_col, inv_lane):
    t = pos_col.shape[0]
    tm = 1024
    return pl.pallas_call(
        _rope_body,
        out_shape=jax.ShapeDtypeStruct((t, ROPE_TW), F32),
        grid=(t // tm,),
        in_specs=[pl.BlockSpec((tm, 1), lambda i: (i, 0)),
                  pl.BlockSpec((1, LANES), lambda i: (0, 0))],
        out_specs=pl.BlockSpec((tm, ROPE_TW), lambda i: (i, 0)),
        compiler_params=_cparams(("parallel",), 16),
        name="rope_table",
    )(pos_col, inv_lane)


def _ffn_body(x_ref, g_ref, wg_ref, wu_ref, wd_ref, gf_ref, o_ref, xn_ref, *, final):
    j = pl.program_id(1)

    @pl.when(j == 0)
    def _():
        x = x_ref[...]
        xn_ref[...] = _rms(x, g_ref[...]).astype(BF16)
        o_ref[...] = x

    xn = xn_ref[...]
    gate = _dot(xn, wg_ref[...])
    up = _dot(xn, wu_ref[...])
    act = (gate * jax.nn.sigmoid(gate)) * (up * 0.5)
    o_ref[...] += _dot(act.astype(BF16), wd_ref[...])

    if final:
        @pl.when(j == pl.num_programs(1) - 1)
        def _():
            o_ref[...] = _rms(o_ref[...], gf_ref[...])


def _ffn(h, g, wg, wu, wd, layer, g_final, final):
    t, d = h.shape
    f = wg.shape[2]
    tm, tf = FFN_TM, FFN_TF
    return pl.pallas_call(
        functools.partial(_ffn_body, final=final),
        out_shape=jax.ShapeDtypeStruct((t, d), F32),
        grid=(t // tm, f // tf),
        in_specs=[pl.BlockSpec((tm, d), lambda i, j: (i, 0)),
                  pl.BlockSpec((1, d), lambda i, j: (0, 0)),
                  pl.BlockSpec((None, d, tf), lambda i, j: (layer, 0, j)),
                  pl.BlockSpec((None, d, tf), lambda i, j: (layer, 0, j)),
                  pl.BlockSpec((None, tf, d), lambda i, j: (layer, j, 0)),
                  pl.BlockSpec((1, d), lambda i, j: (0, 0))],
        out_specs=pl.BlockSpec((tm, d), lambda i, j: (i, 0)),
        scratch_shapes=[pltpu.VMEM((tm, d), BF16)],
        compiler_params=_cparams(("parallel", "arbitrary"), 60),
        name="ffn",
    )(h, g, wg, wu, wd, g_final)


def _inproj_body(x_ref, g_ref, w_ref, cq_ref, ckv_ref, pbm_ref, pbl_ref, pc_ref):
    xn = _rms(x_ref[...], g_ref[...]).astype(BF16)
    off = 0
    for ref in (cq_ref, ckv_ref, pbm_ref, pbl_ref, pc_ref):
        n = ref.shape[-1]
        ref[...] = _dot(xn, w_ref[:, off:off + n])
        off += n


def _in_proj(h, g, w):
    t, d = h.shape
    tm = PROJ_TM
    return pl.pallas_call(
        _inproj_body,
        out_shape=[jax.ShapeDtypeStruct((t, n), F32) for n in P_SPLITS],
        grid=(t // tm,),
        in_specs=[pl.BlockSpec((tm, d), lambda i: (i, 0)),
                  pl.BlockSpec((1, d), lambda i: (0, 0)),
                  pl.BlockSpec((d, P_TOTAL), lambda i: (0, 0), pipeline_mode=pl.Buffered(1))],
        out_specs=[pl.BlockSpec((tm, n), lambda i: (i, 0)) for n in P_SPLITS],
        compiler_params=_cparams(("parallel",), 56),
        name="in_proj",
    )(h, g, w)


def _mla_prep_body(cq_ref, ckv_ref, tab_ref, qn_ref, kvn_ref, wq_ref, wkv_ref, q_ref, k_ref, v_ref):
    nq = A_HEADS * NOPE_DIM
    nr = A_HEADS * ROPE_DIM
    scale = QK_DIM ** -0.5 * LOG2_E
    tab = tab_ref[0]
    cos2 = tab[:, 0:LANES]
    sin2 = tab[:, LANES:2 * LANES]
    cs = tab[:, 2 * LANES:3 * LANES]

    cqn = _rms(cq_ref[0], qn_ref[...]).astype(BF16)
    qa = _dot(cqn, wq_ref[...])
    cos_h = jnp.concatenate([cos2] * (nr // LANES), axis=1)
    sin_h = jnp.concatenate([sin2] * (nr // LANES), axis=1)
    q_rot = qa[:, nq:nq + nr] * cos_h + qa[:, nq + nr:nq + 2 * nr] * sin_h

    ckv = ckv_ref[0]
    ckvn = _rms(ckv[:, :KV_LORA], kvn_ref[...]).astype(BF16)
    kv = _dot(ckvn, wkv_ref[...])
    t = ckv[:, KV_LORA:KV_LORA + LANES] * cs
    low_half = lax.broadcasted_iota(jnp.int32, t.shape, 1) < ROPE_DIM
    k_rot = jnp.where(low_half, t + pltpu.roll(t, ROPE_DIM, axis=1), 0.0).astype(BF16)
    for h in range(A_HEADS):
        pair = q_rot[:, (h // 2) * LANES:(h // 2 + 1) * LANES]
        if h % 2:
            pair = pltpu.roll(pair, ROPE_DIM, axis=1)
        q_rope = jnp.where(low_half, pair, 0.0)
        q_nope = qa[:, h * NOPE_DIM:(h + 1) * NOPE_DIM]
        q_ref[0, h] = (jnp.concatenate([q_nope, q_rope], axis=1) * scale).astype(BF16)
        k_ref[0, h] = jnp.concatenate([kv[:, h * NOPE_DIM:(h + 1) * NOPE_DIM].astype(BF16), k_rot], axis=1)
        v_ref[0, h] = kv[:, nq + h * V_DIM:nq + (h + 1) * V_DIM].astype(BF16)


def _mla_prep_specs(b, s, ts, qn, kvn, wq, wkv):
    tok = lambda n: pl.BlockSpec((1, ts, n), lambda bi, i: (bi, i, 0))
    full = lambda a: pl.BlockSpec(a.shape, lambda bi, i: (0, 0))
    head = lambda n: pl.BlockSpec((1, A_HEADS, ts, n), lambda bi, i: (bi, 0, i, 0))
    in_specs = [tok(Q_LORA), tok(CKV_W), tok(ROPE_TW), full(qn), full(kvn), full(wq), full(wkv)]
    out_shape = [jax.ShapeDtypeStruct((b, A_HEADS, s, QK_PAD), BF16),
                 jax.ShapeDtypeStruct((b, A_HEADS, s, QK_PAD), BF16),
                 jax.ShapeDtypeStruct((b, A_HEADS, s, V_DIM), BF16)]
    return in_specs, out_shape, [head(QK_PAD), head(QK_PAD), head(V_DIM)]


def _attn_body(q_ref, k_ref, v_ref, g_ref, o_ref, s_ref, mx_ref, m_ref, acc_ref):
    i = pl.program_id(1)
    nh = q_ref.shape[1]
    tq = q_ref.shape[2]
    heads = range(nh)
    ones = jnp.ones((tq, V_DIM), BF16)
    keys = lambda j: pl.ds(pl.multiple_of(j * tq, tq), tq)

    row = lax.broadcasted_iota(jnp.int32, (tq, tq), 0)
    col = lax.broadcasted_iota(jnp.int32, (tq, tq), 1)
    for h in heads:
        s = jnp.where(col <= row, _dot_nt(q_ref[0, h], k_ref[0, h, keys(i), :]), -1e30)
        s_ref[h, i] = s
        mx_ref[h] = s

    def scores(j, _):
        for h in heads:
            s = _dot_nt(q_ref[0, h], k_ref[0, h, keys(j), :])
            s_ref[h, j] = s
            mx_ref[h] = jnp.maximum(mx_ref[h], s)
        return 0

    lax.fori_loop(0, i, scores, 0)
    for h in heads:
        m_ref[h] = jnp.broadcast_to(jnp.max(mx_ref[h], axis=-1, keepdims=True), (tq, LANES))

    def weighted(j, h):
        m = m_ref[h]
        p = jnp.exp2(s_ref[h, j] - jnp.concatenate([m] * (tq // LANES), axis=1))
        return _dot(p.astype(BF16), jnp.concatenate([v_ref[0, h, keys(j), :], ones], axis=1))

    for h in heads:
        acc_ref[h] = weighted(i, h)

    def values(j, _):
        for h in heads:
            acc_ref[h] += weighted(j, h)
        return 0

    lax.fori_loop(0, i, values, 0)
    for h in heads:
        acc = acc_ref[h]
        hs = slice(h * V_DIM, (h + 1) * V_DIM)
        o_ref[0, :, hs] = _rms(acc[:, :V_DIM] / acc[:, V_DIM:], g_ref[:, hs]).astype(o_ref.dtype)


def _attention(q, k, v, g):
    b, nh, s, _ = q.shape
    tq = ATTN_TQ
    return pl.pallas_call(
        _attn_body,
        out_shape=jax.ShapeDtypeStruct((b, s, nh * V_DIM), BF16),
        grid=(b, s // tq),
        in_specs=[pl.BlockSpec((1, nh, tq, QK_PAD), lambda bi, i: (bi, 0, i, 0)),
                  pl.BlockSpec((1, nh, s, QK_PAD), lambda bi, i: (bi, 0, 0, 0)),
                  pl.BlockSpec((1, nh, s, V_DIM), lambda bi, i: (bi, 0, 0, 0)),
                  pl.BlockSpec((1, nh * V_DIM), lambda bi, i: (0, 0))],
        out_specs=pl.BlockSpec((1, tq, nh * V_DIM), lambda bi, i: (bi, i, 0)),
        scratch_shapes=[pltpu.VMEM((nh, s // tq, tq, tq), F32),
                        pltpu.VMEM((nh, tq, tq), F32),
                        pltpu.VMEM((nh, tq, LANES), F32),
                        pltpu.VMEM((nh, tq, 2 * V_DIM), F32)],
        compiler_params=_cparams(("parallel", "arbitrary"), 56),
        name="mla_attention",
    )(q, k, v, g)


def _rwkv_prep_body(pbm_ref, pbl_ref, pbm_prev_ref, pbl_prev_ref, mum_ref, mul_ref, w0_ref, a0_ref,
                    dec_ref, icl_ref, gat_ref, kk_ref, ka_ref, rk_ref,
                    at_ref, rt_ref, bt_ref, kt_ref, bh_ref, kh_ref, v_ref, g_ref, bonus_ref, ptot_ref):
    i = pl.program_id(1)
    ts = pbm_ref.shape[1]
    w = B_WIDTH
    first = i == 0

    xm = pbm_ref[0]
    xl = pbl_ref[0]
    pm = jnp.where(first, 0.0, pbm_prev_ref[0])
    pv = jnp.where(first, 0.0, pbl_prev_ref[0])
    xs = xm + (_shift_rows(pm, xm, 1) - xm) * mum_ref[...]
    ls = xl + (_shift_rows(pv, xl, 1) - xl) * mul_ref[...]
    r = xs[:, 0:w]
    k = xs[:, w:2 * w]
    v = xs[:, 2 * w:3 * w]

    ls_wa = ls[:, :LANES]
    wl = w0_ref[...] + _dot3(jnp.tanh(ls_wa), dec_ref[...])
    lw = (-DECAY_LOG_BOUND * LOG2_E) * jax.nn.sigmoid(wl)
    a_ic = jax.nn.sigmoid(a0_ref[...] + _dot3(ls_wa, icl_ref[...]))
    gate = _dot3(jax.nn.sigmoid(ls[:, LANES:]), gat_ref[...])

    seg2 = _group_ones2(w, B_HEAD)
    kk = k * kk_ref[...]
    kkn = kk * jnp.minimum(lax.rsqrt(_dot_exact_rhs(kk * kk, seg2)), 1e12)
    kp = k * (1.0 + (a_ic - 1.0) * ka_ref[...])
    bonus = _dot_exact_rhs(r * kp * rk_ref[...], seg2) * v

    ct = CUM_TILE
    row = lax.broadcasted_iota(jnp.int32, (2 * ct, 2 * ct), 0)
    col = lax.broadcasted_iota(jnp.int32, (2 * ct, 2 * ct), 1) & (ct - 1)
    same = ((row & (ct - 1)) >> CHUNK_SHIFT) == (col >> CHUNK_SHIFT)
    pos = row & (ct - 1)
    first_col = jnp.where(row < ct, 0, pos + 1)
    last_col = jnp.where(row < ct, pos, ct)
    tri2 = (same & (col >= first_col) & (col <= last_col)).astype(BF16)
    lw_hi, lw_lo = _split(lw)
    cums = [_dot(tri2, jnp.concatenate([lw_hi[t0:t0 + ct], lw_lo[t0:t0 + ct]], axis=0))
            for t0 in range(0, ts, ct)]
    cum = jnp.concatenate([c[:ct] for c in cums], axis=0)
    cumr = jnp.concatenate([c[ct:] for c in cums], axis=0)

    e_neg = jnp.exp2(-cum)
    e_rem = jnp.exp2(cumr)
    b_vec = kkn * a_ic
    at_ref[0] = (-kkn * jnp.exp2(cum - lw)).astype(BF16)
    rt_ref[0] = (r * jnp.exp2(cum)).astype(BF16)
    bt_ref[0] = (b_vec * e_neg).astype(BF16)
    kt_ref[0] = (kp * e_neg).astype(BF16)
    bh_ref[0] = (b_vec * e_rem).astype(BF16)
    kh_ref[0] = (kp * e_rem).astype(BF16)
    v_ref[0] = v.astype(BF16)
    g_ref[0] = gate
    bonus_ref[0] = bonus
    nc = ptot_ref.shape[2]
    for t in range(ptot_ref.shape[1]):
        last_rows = [(t * nc + c + 1) * CHUNK - 1 for c in range(nc)]
        ptot_ref[0, t] = jnp.exp2(jnp.concatenate([cum[r0:r0 + 1] for r0 in last_rows], axis=0))


def _rwkv_prep_specs(b, s, ts, params):
    nc = SCAN_TILE // CHUNK
    nt = ts // SCAN_TILE
    w = B_WIDTH
    tok = lambda n: pl.BlockSpec((1, ts, n), lambda bi, i: (bi, i, 0))
    prev = lambda n: pl.BlockSpec(
        (1, SUBLANES, n), lambda bi, i: (bi, jnp.maximum(i * (ts // SUBLANES) - 1, 0), 0))
    full = lambda a: pl.BlockSpec(a.shape, lambda bi, i: (0, 0))
    in_specs = [tok(PBM_W), tok(PBL_W), prev(PBM_W), prev(PBL_W)] + [full(p) for p in params]
    out_shape = ([jax.ShapeDtypeStruct((b, s, w), BF16)] * 7 + [jax.ShapeDtypeStruct((b, s, w), F32)] * 2
                 + [jax.ShapeDtypeStruct((b, s // SCAN_TILE, nc, w), F32)])
    out_specs = [tok(w)] * 9 + [pl.BlockSpec((1, nt, nc, w), lambda bi, i: (bi, i, 0, 0))]
    return in_specs, out_shape, out_specs


def _rwkv_scan_body(at_ref, rt_ref, bt_ref, kt_ref, bh_ref, kh_ref, v_ref, g_ref, bonus_ref, ptot_ref,
                    gain_ref, bias_ref, o_ref, state_ref):
    n = B_HEAD
    c_len = CHUNK
    ts = at_ref.shape[1]

    @pl.when(pl.program_id(1) == 0)
    def _():
        state_ref[...] = jnp.zeros_like(state_ref)

    row = lax.broadcasted_iota(jnp.int32, (c_len, 2 * c_len), 0)
    col = lax.broadcasted_iota(jnp.int32, (c_len, 2 * c_len), 1) & (c_len - 1)
    strict2 = col < row
    incl2 = col <= row
    eye = (lax.broadcasted_iota(jnp.int32, (n, n), 0) == lax.broadcasted_iota(jnp.int32, (n, n), 1))
    eye_f = eye.astype(F32)
    zero_blk = jnp.zeros((c_len, n), BF16)
    sq_row = lax.broadcasted_iota(jnp.int32, (c_len, c_len), 0)
    sq_col = lax.broadcasted_iota(jnp.int32, (c_len, c_len), 1)
    same_block = lambda bits: (sq_row >> bits) == (sq_col >> bits)
    diag_blocks = same_block(INV_BASE_BITS)
    merge_masks = [same_block(bits + 1) & ~same_block(bits) for bits in range(INV_BASE_BITS, CHUNK_SHIFT)]

    n_chunks = ts // c_len
    heads = range(at_ref.shape[2] // n)
    units = [(c, h) for c in range(n_chunks) for h in heads]
    rows = lambda c: slice(c * c_len, (c + 1) * c_len)
    lanes = lambda h: slice(h * n, (h + 1) * n)
    load = lambda ref: {(c, h): ref[0, rows(c), lanes(h)] for c, h in units}
    at, rt, bt, kt, bh, kh, vv = (load(r) for r in (at_ref, rt_ref, bt_ref, kt_ref, bh_ref, kh_ref, v_ref))
    state = [state_ref[h] for h in heads]

    g1 = {u: _dot_nt(jnp.concatenate([at[u], rt[u]], axis=0), jnp.concatenate([bt[u], kt[u]], axis=0))
          for u in units}
    a_low = {u: jnp.where(strict2, g1[u][:c_len], 0.0) for u in units}
    r_low = {u: jnp.where(incl2, g1[u][c_len:], 0.0).astype(BF16) for u in units}

    l_mat = {u: a_low[u][:, :c_len] for u in units}
    p = {u: jnp.where(diag_blocks, l_mat[u], 0.0) for u in units}
    t_inv = {u: eye_f + p[u] for u in units}
    p = {u: p[u].astype(BF16) for u in units}
    for _ in range(INV_BASE_BITS - 1):
        p = {u: _dot(p[u], p[u]).astype(BF16) for u in units}
        t_inv = {u: t_inv[u] + _dot(t_inv[u].astype(BF16), p[u]) for u in units}
    for off_mask in merge_masks:
        t_b = {u: t_inv[u].astype(BF16) for u in units}
        t_l = {u: _dot(t_b[u], jnp.where(off_mask, l_mat[u], 0.0).astype(BF16)).astype(BF16) for u in units}
        t_inv = {u: t_inv[u] + _dot(t_l[u], t_b[u]) for u in units}

    akv = {u: _dot(a_low[u][:, c_len:].astype(BF16), vv[u]).astype(BF16) for u in units}
    x = {u: _dot(t_inv[u].astype(BF16), jnp.concatenate([at[u], akv[u]], axis=1)).astype(BF16) for u in units}
    rhs = {u: jnp.concatenate([x[u], jnp.concatenate([zero_blk, vv[u]], axis=1)], axis=0) for u in units}
    qy = {u: _dot(r_low[u], rhs[u]) for u in units}
    wmat = {u: _dot_tn(rhs[u], jnp.concatenate([bh[u], kh[u]], axis=0)) for u in units}
    mc = {(c, h): (jnp.where(eye, ptot_ref[0, 0, c:c + 1, lanes(h)], 0.0) + wmat[(c, h)][:n]).astype(BF16)
          for c, h in units}
    rp = {u: (rt[u].astype(F32) + qy[u][:, :n]).astype(BF16) for u in units}

    y = {}
    for c, h in units:
        s0 = state[h].astype(BF16)
        y[(c, h)] = _dot_nt(rp[(c, h)], s0) + qy[(c, h)][:, n:]
        state[h] = _dot(s0, mc[(c, h)]) + wmat[(c, h)][n:]

    for c in range(n_chunks):
        outs = []
        for h in heads:
            yv = y[(c, h)]
            mu = jnp.mean(yv, axis=-1, keepdims=True)
            yc = yv - mu
            var = jnp.mean(yc * yc, axis=-1, keepdims=True)
            yn = yc * lax.rsqrt(var + HEAD_NORM_EPS) * gain_ref[:, lanes(h)] + bias_ref[:, lanes(h)]
            outs.append((yn + bonus_ref[0, rows(c), lanes(h)]) * g_ref[0, rows(c), lanes(h)])
        o_ref[0, rows(c), :] = jnp.concatenate(outs, axis=1).astype(o_ref.dtype)
    for h in heads:
        state_ref[h] = state[h]


def _rwkv_scan(arrs, gain, bias):
    at = arrs[0]
    b, s, w = at.shape
    ts = SCAN_TILE
    nc = ts // CHUNK
    tok = pl.BlockSpec((1, ts, w), lambda bi, i: (bi, i, 0))
    vec = pl.BlockSpec((1, w), lambda bi, i: (0, 0))
    return pl.pallas_call(
        _rwkv_scan_body,
        out_shape=jax.ShapeDtypeStruct((b, s, w), BF16),
        grid=(b, s // ts),
        in_specs=[tok] * 9 + [pl.BlockSpec((1, 1, nc, w), lambda bi, i: (bi, i, 0, 0)), vec, vec],
        out_specs=tok,
        scratch_shapes=[pltpu.VMEM((w // B_HEAD, B_HEAD, B_HEAD), F32)],
        compiler_params=_cparams(("parallel", "arbitrary"), 32),
        name="rwkv_scan",
    )(*arrs, gain, bias)


def _conv_body(pc_ref, prev_ref, w_ref, g_ref, o_ref):
    w = C_WIDTH
    x = pc_ref[0]
    xp = jnp.where(pl.program_id(1) == 0, 0.0, prev_ref[0])
    u = x[:, w:2 * w] * x[:, 2 * w:3 * w]
    up = xp[:, w:2 * w] * xp[:, 2 * w:3 * w]
    cw = w_ref[...]
    y = cw[0:1] * _shift_rows(up, u, 2) + cw[1:2] * _shift_rows(up, u, 1) + cw[2:3] * u
    z = x[:, 0:w] * y
    ms = _dot_exact_rhs(z * z, _group_ones2(w, w // C_GROUPS)) * (C_GROUPS / w)
    o_ref[0] = (z * lax.rsqrt(ms + RMS_EPS) * g_ref[...]).astype(o_ref.dtype)


def _conv_specs(b, s, ts, conv_w, gain):
    in_specs = [pl.BlockSpec((1, ts, PC_W), lambda bi, i: (bi, i, 0)),
                pl.BlockSpec((1, SUBLANES, PC_W),
                             lambda bi, i: (bi, jnp.maximum(i * (ts // SUBLANES) - 1, 0), 0)),
                pl.BlockSpec(conv_w.shape, lambda bi, i: (0, 0)),
                pl.BlockSpec(gain.shape, lambda bi, i: (0, 0))]
    out_shape = [jax.ShapeDtypeStruct((b, s, C_WIDTH), BF16)]
    return in_specs, out_shape, [pl.BlockSpec((1, ts, C_WIDTH), lambda bi, i: (bi, i, 0))]


def _mixer_prep_body(*refs, n_in, n_out):
    ins, outs = refs[:sum(n_in)], refs[sum(n_in):]
    bodies = (_mla_prep_body, _rwkv_prep_body, _conv_body)
    i0 = o0 = 0
    for body, ni, no in zip(bodies, n_in, n_out):
        body(*ins[i0:i0 + ni], *outs[o0:o0 + no])
        i0 += ni
        o0 += no


def _mixer_prep(cq, ckv, tab, mla_params, pbm, pbl, rw_params, pc, conv_w, conv_gain):
    b, s, _ = cq.shape
    ts = PREP_TILE
    parts = (_mla_prep_specs(b, s, ts, *mla_params), _rwkv_prep_specs(b, s, ts, rw_params),
             _conv_specs(b, s, ts, conv_w, conv_gain))
    outs = pl.pallas_call(
        functools.partial(_mixer_prep_body, n_in=tuple(len(p[0]) for p in parts),
                          n_out=tuple(len(p[1]) for p in parts)),
        out_shape=[sh for p in parts for sh in p[1]],
        grid=(b, s // ts),
        in_specs=[sp for p in parts for sp in p[0]],
        out_specs=[sp for p in parts for sp in p[2]],
        compiler_params=_cparams(("parallel", "parallel"), 60),
        name="mixer_prep",
    )(cq, ckv, tab, *mla_params, pbm, pbl, pbm, pbl, *rw_params, pc, pc, conv_w, conv_gain)
    return outs[:3], outs[3:13], outs[13]


def _outproj_body(h_ref, ya_ref, yb_ref, yc_ref, w_ref, o_ref):
    acc = h_ref[...] + _dot(ya_ref[...], w_ref[0:A_WIDTH, :])
    acc = acc + _dot(yb_ref[...], w_ref[A_WIDTH:A_WIDTH + B_WIDTH, :])
    o_ref[...] = acc + _dot(yc_ref[...], w_ref[A_WIDTH + B_WIDTH:, :])


def _out_proj(h, ya, yb, yc, w, layer):
    t, d = h.shape
    tm = PROJ_TM
    row = lambda n: pl.BlockSpec((tm, n), lambda i: (i, 0))
    return pl.pallas_call(
        _outproj_body,
        out_shape=jax.ShapeDtypeStruct((t, d), F32),
        grid=(t // tm,),
        in_specs=[row(d), row(A_WIDTH), row(B_WIDTH), row(C_WIDTH),
                  pl.BlockSpec((None,) + w.shape[1:], lambda i: (layer, 0, 0))],
        out_specs=row(d),
        compiler_params=_cparams(("parallel",), 48),
        name="out_proj",
    )(h, ya, yb, yc, w)


def _swap_halves(w):
    half = ROPE_DIM // 2
    return jnp.concatenate([w[..., half:], w[..., :half]], axis=-1)


def _arrange_w_in(w_in):
    w_in = w_in.astype(BF16)
    o = Q_LORA + KV_LORA
    k_rope = w_in[:, o:A_IN]
    b0 = A_IN
    return jnp.concatenate(
        [w_in[:, :Q_LORA], w_in[:, Q_LORA:o], k_rope, _swap_halves(k_rope),
         w_in[:, b0:b0 + PBM_W], _spread_lora(w_in[:, b0 + PBM_W:b0 + B_IN]), w_in[:, b0 + B_IN:]], axis=1)


def _spread_lora(cols):
    n_wa = DECAY_LORA + ICLR_LORA
    zeros = lambda n: jnp.zeros(cols.shape[:-1] + (n,), cols.dtype)
    return jnp.concatenate([cols[..., :n_wa], zeros(LANES - n_wa), cols[..., n_wa:], zeros(LANES - GATE_LORA)],
                           axis=-1)


def _arrange_w_uq(w_uq):
    w = w_uq.astype(BF16).reshape(Q_LORA, A_HEADS, QK_DIM)
    nope = w[:, :, :NOPE_DIM].reshape(Q_LORA, -1)
    rope = w[:, :, NOPE_DIM:]
    return jnp.concatenate(
        [nope, rope.reshape(Q_LORA, -1), _swap_halves(rope).reshape(Q_LORA, -1)], axis=1)


def _arrange_w_ukv(w_ukv):
    w = w_ukv.astype(BF16).reshape(KV_LORA, A_HEADS, NOPE_DIM + V_DIM)
    return jnp.concatenate(
        [w[:, :, :NOPE_DIM].reshape(KV_LORA, -1), w[:, :, NOPE_DIM:].reshape(KV_LORA, -1)], axis=1)


def _pad_rows(w, start):
    full = jnp.zeros((LANES, w.shape[1]), F32).at[start:start + w.shape[0]].set(w)
    hi = full.astype(BF16)
    return jnp.concatenate([hi, hi, (full - hi.astype(F32)).astype(BF16)], axis=0)


def _row(v):
    return v.reshape(1, -1)


def kernel(x, positions, norm_ffn1, ffn1_gate, ffn1_up, ffn1_down, norm_mix, w_in, q_norm, kv_norm, w_uq, w_ukv, attn_out_norm, shift_mu, decay_w0, decay_up, iclr_a0, iclr_up, gate_up, k_k, k_a, r_k, lnx_gain, lnx_bias, conv_w, conv_out_norm, w_out, norm_ffn2, ffn2_gate, ffn2_up, ffn2_down, norm_final):
    b, s, d = x.shape
    t = b * s
    seq = lambda a: a.reshape(b, s, a.shape[-1])

    inv_freq = 1.0 / (ROPE_THETA ** (jnp.arange(0, ROPE_DIM, 2, dtype=F32) / ROPE_DIM))
    inv_lane = jnp.tile(inv_freq, LANES // inv_freq.shape[0]).reshape(1, LANES)
    tab = seq(_rope_table(positions.reshape(t, 1), inv_lane))

    h = x.reshape(t, d)
    g_final = _row(norm_final)
    ffn1 = tuple(w.astype(BF16) for w in (ffn1_gate, ffn1_up, ffn1_down))
    ffn2 = tuple(w.astype(BF16) for w in (ffn2_gate, ffn2_up, ffn2_down))
    w_out_b = w_out.astype(BF16)
    for l in range(DEPTH):
        h = _ffn(h, _row(norm_ffn1[l]), *ffn1, l, g_final, False)

        cq, ckv, pbm, pbl, pc = _in_proj(h, _row(norm_mix[l]), _arrange_w_in(w_in[l]))

        mu = shift_mu[l]
        mu_l = _spread_lora(mu[PBM_W:])
        mla_params = (_row(q_norm[l]), _row(kv_norm[l]), _arrange_w_uq(w_uq[l]), _arrange_w_ukv(w_ukv[l]))
        rw_params = (_row(mu[:PBM_W]), _row(mu_l), _row(decay_w0[l]), _row(iclr_a0[l]),
                     _pad_rows(decay_up[l], 0), _pad_rows(iclr_up[l], DECAY_LORA), _pad_rows(gate_up[l], 0),
                     _row(k_k[l]), _row(k_a[l]), _row(r_k[l]))
        (q, k, v), rw, yc = _mixer_prep(seq(cq), seq(ckv), tab, mla_params, seq(pbm), seq(pbl), rw_params,
                                        seq(pc), conv_w[l], _row(conv_out_norm[l]))
        ya = _attention(q, k, v, _row(attn_out_norm[l]))
        yb = _rwkv_scan(rw, _row(lnx_gain[l]), _row(lnx_bias[l]))

        h = _out_proj(h, ya.reshape(t, A_WIDTH), yb.reshape(t, B_WIDTH), yc.reshape(t, C_WIDTH), w_out_b, l)

        h = _ffn(h, _row(norm_ffn2[l]), *ffn2, l, g_final, l == DEPTH - 1)
    return h.reshape(b, s, d)
```

```python
import functools

import jax
import jax.numpy as jnp
from jax import lax
from jax.experimental import pallas as pl
from jax.experimental.pallas import tpu as pltpu

F32 = jnp.float32
BF16 = jnp.bfloat16

D_MODEL = 2048
DEPTH = 4
A_WIDTH = 1024
B_WIDTH = 512
C_WIDTH = 512
V_DIM = 128
NOPE_DIM = 128
ROPE_DIM = 64
QK_DIM = NOPE_DIM + ROPE_DIM
QK_PAD = 256
A_HEADS = 8
Q_LORA = 512
KV_LORA = 256
ROPE_THETA = 10000.0
B_HEAD = 64
B_HEADS = 8
DECAY_LORA = 32
ICLR_LORA = 32
GATE_LORA = 96
HEAD_NORM_EPS = 64e-5
C_GROUPS = 8
CONV_K = 3
D_FF = 5632
RMS_EPS = 1e-6
LOG2_E = 1.4426950408889634
DECAY_LOG_BOUND = 0.6065306597126334
A_IN = Q_LORA + KV_LORA + ROPE_DIM
B_IN = 3 * B_WIDTH + DECAY_LORA + ICLR_LORA + GATE_LORA

CKV_W = KV_LORA + 2 * ROPE_DIM
PBM_W = 3 * B_WIDTH
PBL_W = 256
PC_W = 3 * C_WIDTH
P_SPLITS = (Q_LORA, CKV_W, PBM_W, PBL_W, PC_W)
P_TOTAL = sum(P_SPLITS)

V7X_VMEM_BYTES = 64 * 1024 * 1024
SUBLANES = 8
LANES = 128
FFN_TM = 1024
FFN_TF = 512
PROJ_TM = 512
SCAN_TILE = 512
PREP_TILE = 512
CUM_TILE = 256
ATTN_TQ = 256
CHUNK = 64
CHUNK_SHIFT = CHUNK.bit_length() - 1
INV_BASE_BITS = 3
ROPE_TW = 3 * LANES


def _cparams(sem, vmem_mb):
    return pltpu.CompilerParams(dimension_semantics=sem, vmem_limit_bytes=vmem_mb * 1024 * 1024)


def _rms(x, g, eps=RMS_EPS):
    return x * lax.rsqrt(jnp.mean(x * x, axis=-1, keepdims=True) + eps) * g


def _dot(a, b):
    return jnp.dot(a, b, preferred_element_type=F32)


def _dot_nt(a, b):
    return lax.dot_general(a, b, (((1,), (1,)), ((), ())), preferred_element_type=F32)


def _dot_tn(a, b):
    return lax.dot_general(a, b, (((0,), (0,)), ((), ())), preferred_element_type=F32)


def _split(x):
    hi = x.astype(BF16)
    lo = (x - hi.astype(F32)).astype(BF16)
    return hi, lo


def _dot_exact_rhs(x, m2):
    hi, lo = _split(x)
    return _dot(jnp.concatenate([hi, lo], axis=1), m2)


def _dot3(x, w3):
    hi, lo = _split(x)
    return _dot(jnp.concatenate([hi, lo, hi], axis=1), w3)


def _shift_rows(prev8, x, k):
    cat = jnp.concatenate([prev8, x], axis=0)
    return pltpu.roll(cat, k, axis=0)[SUBLANES:]


def _group_ones2(n, group):
    shift = group.bit_length() - 1
    r = (lax.broadcasted_iota(jnp.int32, (2 * n, n), 0) & (n - 1)) >> shift
    c = lax.broadcasted_iota(jnp.int32, (2 * n, n), 1) >> shift
    return (r == c).astype(BF16)


def _rope_body(pos_ref, inv_ref, o_ref):
    ang = pos_ref[...].astype(F32) * inv_ref[...]
    c = jnp.cos(ang)
    s = jnp.sin(ang)
    lane = lax.broadcasted_iota(jnp.int32, ang.shape, 1)
    ssg = jnp.where((lane & (ROPE_DIM - 1)) < ROPE_DIM // 2, -s, s)
    o_ref[:, 0:LANES] = c
    o_ref[:, LANES:2 * LANES] = ssg
    o_ref[:, 2 * LANES:3 * LANES] = jnp.where(lane < ROPE_DIM, c, ssg)


def _rope_table(pos_col, inv_lane):
    t = pos_col.shape[0]
    tm = 1024
    return pl.pallas_call(
        _rope_body,
        out_shape=jax.ShapeDtypeStruct((t, ROPE_TW), F32),
        grid=(t // tm,),
        in_specs=[pl.BlockSpec((tm, 1), lambda i: (i, 0)),
                  pl.BlockSpec((1, LANES), lambda i: (0, 0))],
        out_specs=pl.BlockSpec((tm, ROPE_TW), lambda i: (i, 0)),
        compiler_params=_cparams(("parallel",), 16),
        name="rope_table",
    )(pos_col, inv_lane)


def _ffn_body(x_ref, g_ref, wg_ref, wu_ref, wd_ref, gf_ref, o_ref, xn_ref, *, final):
    j = pl.program_id(1)

    @pl.when(j == 0)
    def _():
        x = x_ref[...]
        xn_ref[...] = _rms(x, g_ref[...]).astype(BF16)
        o_ref[...] = x

    xn = xn_ref[...]
    gate = _dot(xn, wg_ref[...])
    up = _dot(xn, wu_ref[...])
    act = (gate * jax.nn.sigmoid(gate)) * (up * 0.5)
    o_ref[...] += _dot(act.astype(BF16), wd_ref[...])

    if final:
        @pl.when(j == pl.num_programs(1) - 1)
        def _():
            o_ref[...] = _rms(o_ref[...], gf_ref[...])


def _ffn(h, g, wg, wu, wd, layer, g_final, final):
    t, d = h.shape
    f = wg.shape[2]
    tm, tf = FFN_TM, FFN_TF
    return pl.pallas_call(
        functools.partial(_ffn_body, final=final),
        out_shape=jax.ShapeDtypeStruct((t, d), F32),
        grid=(t // tm, f // tf),
        in_specs=[pl.BlockSpec((tm, d), lambda i, j: (i, 0)),
                  pl.BlockSpec((1, d), lambda i, j: (0, 0)),
                  pl.BlockSpec((None, d, tf), lambda i, j: (layer, 0, j)),
                  pl.BlockSpec((None, d, tf), lambda i, j: (layer, 0, j)),
                  pl.BlockSpec((None, tf, d), lambda i, j: (layer, j, 0)),
                  pl.BlockSpec((1, d), lambda i, j: (0, 0))],
        out_specs=pl.BlockSpec((tm, d), lambda i, j: (i, 0)),
        scratch_shapes=[pltpu.VMEM((tm, d), BF16)],
        compiler_params=_cparams(("parallel", "arbitrary"), 60),
        name="ffn",
    )(h, g, wg, wu, wd, g_final)


def _inproj_body(x_ref, g_ref, w_ref, cq_ref, ckv_ref, pbm_ref, pbl_ref, pc_ref):
    xn = _rms(x_ref[...], g_ref[...]).astype(BF16)
    off = 0
    for ref in (cq_ref, ckv_ref, pbm_ref, pbl_ref, pc_ref):
        n = ref.shape[-1]
        ref[...] = _dot(xn, w_ref[:, off:off + n])
        off += n


def _in_proj(h, g, w):
    t, d = h.shape
    tm = PROJ_TM
    return pl.pallas_call(
        _inproj_body,
        out_shape=[jax.ShapeDtypeStruct((t, n), F32) for n in P_SPLITS],
        grid=(t // tm,),
        in_specs=[pl.BlockSpec((tm, d), lambda i: (i, 0)),
                  pl.BlockSpec((1, d), lambda i: (0, 0)),
                  pl.BlockSpec((d, P_TOTAL), lambda i: (0, 0), pipeline_mode=pl.Buffered(1))],
        out_specs=[pl.BlockSpec((tm, n), lambda i: (i, 0)) for n in P_SPLITS],
        compiler_params=_cparams(("parallel",), 56),
        name="in_proj",
    )(h, g, w)


def _mla_prep_body(cq_ref, ckv_ref, tab_ref, qn_ref, kvn_ref, wq_ref, wkv_ref, q_ref, k_ref, v_ref):
    nq = A_HEADS * NOPE_DIM
    nr = A_HEADS * ROPE_DIM
    scale = QK_DIM ** -0.5 * LOG2_E
    tab = tab_ref[0]
    cos2 = tab[:, 0:LANES]
    sin2 = tab[:, LANES:2 * LANES]
    cs = tab[:, 2 * LANES:3 * LANES]

    cqn = _rms(cq_ref[0], qn_ref[...]).astype(BF16)
    qa = _dot(cqn, wq_ref[...])
    cos_h = jnp.concatenate([cos2] * (nr // LANES), axis=1)
    sin_h = jnp.concatenate([sin2] * (nr // LANES), axis=1)
    q_rot = qa[:, nq:nq + nr] * cos_h + qa[:, nq + nr:nq + 2 * nr] * sin_h

    ckv = ckv_ref[0]
    ckvn = _rms(ckv[:, :KV_LORA], kvn_ref[...]).astype(BF16)
    kv = _dot(ckvn, wkv_ref[...])
    t = ckv[:, KV_LORA:KV_LORA + LANES] * cs
    low_half = lax.broadcasted_iota(jnp.int32, t.shape, 1) < ROPE_DIM
    k_rot = jnp.where(low_half, t + pltpu.roll(t, ROPE_DIM, axis=1), 0.0).astype(BF16)
    for h in range(A_HEADS):
        pair = q_rot[:, (h // 2) * LANES:(h // 2 + 1) * LANES]
        if h % 2:
            pair = pltpu.roll(pair, ROPE_DIM, axis=1)
        q_rope = jnp.where(low_half, pair, 0.0)
        q_nope = qa[:, h * NOPE_DIM:(h + 1) * NOPE_DIM]
        q_ref[0, h] = (jnp.concatenate([q_nope, q_rope], axis=1) * scale).astype(BF16)
        k_ref[0, h] = jnp.concatenate([kv[:, h * NOPE_DIM:(h + 1) * NOPE_DIM].astype(BF16), k_rot], axis=1)
        v_ref[0, h] = kv[:, nq + h * V_DIM:nq + (h + 1) * V_DIM].astype(BF16)


def _mla_prep_specs(b, s, ts, qn, kvn, wq, wkv):
    tok = lambda n: pl.BlockSpec((1, ts, n), lambda bi, i: (bi, i, 0))
    full = lambda a: pl.BlockSpec(a.shape, lambda bi, i: (0, 0))
    head = lambda n: pl.BlockSpec((1, A_HEADS, ts, n), lambda bi, i: (bi, 0, i, 0))
    in_specs = [tok(Q_LORA), tok(CKV_W), tok(ROPE_TW), full(qn), full(kvn), full(wq), full(wkv)]
    out_shape = [jax.ShapeDtypeStruct((b, A_HEADS, s, QK_PAD), BF16),
                 jax.ShapeDtypeStruct((b, A_HEADS, s, QK_PAD), BF16),
                 jax.ShapeDtypeStruct((b, A_HEADS, s, V_DIM), BF16)]
    return in_specs, out_shape, [head(QK_PAD), head(QK_PAD), head(V_DIM)]


def _attn_body(q_ref, k_ref, v_ref, g_ref, o_ref, s_ref, mx_ref, m_ref, acc_ref):
    i = pl.program_id(1)
    nh = q_ref.shape[1]
    tq = q_ref.shape[2]
    heads = range(nh)
    ones = jnp.ones((tq, V_DIM), BF16)
    keys = lambda j: pl.ds(pl.multiple_of(j * tq, tq), tq)

    row = lax.broadcasted_iota(jnp.int32, (tq, tq), 0)
    col = lax.broadcasted_iota(jnp.int32, (tq, tq), 1)
    for h in heads:
        s = jnp.where(col <= row, _dot_nt(q_ref[0, h], k_ref[0, h, keys(i), :]), -1e30)
        s_ref[h, i] = s
        mx_ref[h] = s

    def scores(j, _):
        for h in heads:
            s = _dot_nt(q_ref[0, h], k_ref[0, h, keys(j), :])
            s_ref[h, j] = s
            mx_ref[h] = jnp.maximum(mx_ref[h], s)
        return 0

    lax.fori_loop(0, i, scores, 0)
    for h in heads:
        m_ref[h] = jnp.broadcast_to(jnp.max(mx_ref[h], axis=-1, keepdims=True), (tq, LANES))

    def weighted(j, h):
        m = m_ref[h]
        p = jnp.exp2(s_ref[h, j] - jnp.concatenate([m] * (tq // LANES), axis=1))
        return _dot(p.astype(BF16), jnp.concatenate([v_ref[0, h, keys(j), :], ones], axis=1))

    for h in heads:
        acc_ref[h] = weighted(i, h)

    def values(j, _):
        for h in heads:
            acc_ref[h] += weighted(j, h)
        return 0

    lax.fori_loop(0, i, values, 0)
    for h in heads:
        acc = acc_ref[h]
        hs = slice(h * V_DIM, (h + 1) * V_DIM)
        o_ref[0, :, hs] = _rms(acc[:, :V_DIM] / acc[:, V_DIM:], g_ref[:, hs]).astype(o_ref.dtype)


def _attention(q, k, v, g):
    b, nh, s, _ = q.shape
    tq = ATTN_TQ
    return pl.pallas_call(
        _attn_body,
        out_shape=jax.ShapeDtypeStruct((b, s, nh * V_DIM), BF16),
        grid=(b, s // tq),
        in_specs=[pl.BlockSpec((1, nh, tq, QK_PAD), lambda bi, i: (bi, 0, i, 0)),
                  pl.BlockSpec((1, nh, s, QK_PAD), lambda bi, i: (bi, 0, 0, 0)),
                  pl.BlockSpec((1, nh, s, V_DIM), lambda bi, i: (bi, 0, 0, 0)),
                  pl.BlockSpec((1, nh * V_DIM), lambda bi, i: (0, 0))],
        out_specs=pl.BlockSpec((1, tq, nh * V_DIM), lambda bi, i: (bi, i, 0)),
        scratch_shapes=[pltpu.VMEM((nh, s // tq, tq, tq), F32),
                        pltpu.VMEM((nh, tq, tq), F32),
                        pltpu.VMEM((nh, tq, LANES), F32),
                        pltpu.VMEM((nh, tq, 2 * V_DIM), F32)],
        compiler_params=_cparams(("parallel", "arbitrary"), 56),
        name="mla_attention",
    )(q, k, v, g)


def _rwkv_prep_body(pbm_ref, pbl_ref, pbm_prev_ref, pbl_prev_ref, mum_ref, mul_ref, w0_ref, a0_ref,
                    dec_ref, icl_ref, gat_ref, kk_ref, ka_ref, rk_ref,
                    at_ref, rt_ref, bt_ref, kt_ref, bh_ref, kh_ref, v_ref, g_ref, bonus_ref, ptot_ref):
    i = pl.program_id(1)
    ts = pbm_ref.shape[1]
    w = B_WIDTH
    first = i == 0

    xm = pbm_ref[0]
    xl = pbl_ref[0]
    pm = jnp.where(first, 0.0, pbm_prev_ref[0])
    pv = jnp.where(first, 0.0, pbl_prev_ref[0])
    xs = xm + (_shift_rows(pm, xm, 1) - xm) * mum_ref[...]
    ls = xl + (_shift_rows(pv, xl, 1) - xl) * mul_ref[...]
    r = xs[:, 0:w]
    k = xs[:, w:2 * w]
    v = xs[:, 2 * w:3 * w]

    ls_wa = ls[:, :LANES]
    wl = w0_ref[...] + _dot3(jnp.tanh(ls_wa), dec_ref[...])
    lw = (-DECAY_LOG_BOUND * LOG2_E) * jax.nn.sigmoid(wl)
    a_ic = jax.nn.sigmoid(a0_ref[...] + _dot3(ls_wa, icl_ref[...]))
    gate = _dot3(jax.nn.sigmoid(ls[:, LANES:]), gat_ref[...])

    seg2 = _group_ones2(w, B_HEAD)
    kk = k * kk_ref[...]
    kkn = kk * jnp.minimum(lax.rsqrt(_dot_exact_rhs(kk * kk, seg2)), 1e12)
    kp = k * (1.0 + (a_ic - 1.0) * ka_ref[...])
    bonus = _dot_exact_rhs(r * kp * rk_ref[...], seg2) * v

    ct = CUM_TILE
    row = lax.broadcasted_iota(jnp.int32, (2 * ct, 2 * ct), 0)
    col = lax.broadcasted_iota(jnp.int32, (2 * ct, 2 * ct), 1) & (ct - 1)
    same = ((row & (ct - 1)) >> CHUNK_SHIFT) == (col >> CHUNK_SHIFT)
    pos = row & (ct - 1)
    first_col = jnp.where(row < ct, 0, pos + 1)
    last_col = jnp.where(row < ct, pos, ct)
    tri2 = (same & (col >= first_col) & (col <= last_col)).astype(BF16)
    lw_hi, lw_lo = _split(lw)
    cums = [_dot(tri2, jnp.concatenate([lw_hi[t0:t0 + ct], lw_lo[t0:t0 + ct]], axis=0))
            for t0 in range(0, ts, ct)]
    cum = jnp.concatenate([c[:ct] for c in cums], axis=0)
    cumr = jnp.concatenate([c[ct:] for c in cums], axis=0)

    e_neg = jnp.exp2(-cum)
    e_rem = jnp.exp2(cumr)
    b_vec = kkn * a_ic
    at_ref[0] = (-kkn * jnp.exp2(cum - lw)).astype(BF16)
    rt_ref[0] = (r * jnp.exp2(cum)).astype(BF16)
    bt_ref[0] = (b_vec * e_neg).astype(BF16)
    kt_ref[0] = (kp * e_neg).astype(BF16)
    bh_ref[0] = (b_vec * e_rem).astype(BF16)
    kh_ref[0] = (kp * e_rem).astype(BF16)
    v_ref[0] = v.astype(BF16)
    g_ref[0] = gate
    bonus_ref[0] = bonus
    nc = ptot_ref.shape[2]
    for t in range(ptot_ref.shape[1]):
        last_rows = [(t * nc + c + 1) * CHUNK - 1 for c in range(nc)]
        ptot_ref[0, t] = jnp.exp2(jnp.concatenate([cum[r0:r0 + 1] for r0 in last_rows], axis=0))


def _rwkv_prep_specs(b, s, ts, params):
    nc = SCAN_TILE // CHUNK
    nt = ts // SCAN_TILE
    w = B_WIDTH
    tok = lambda n: pl.BlockSpec((1, ts, n), lambda bi, i: (bi, i, 0))
    prev = lambda n: pl.BlockSpec(
        (1, SUBLANES, n), lambda bi, i: (bi, jnp.maximum(i * (ts // SUBLANES) - 1, 0), 0))
    full = lambda a: pl.BlockSpec(a.shape, lambda bi, i: (0, 0))
    in_specs = [tok(PBM_W), tok(PBL_W), prev(PBM_W), prev(PBL_W)] + [full(p) for p in params]
    out_shape = ([jax.ShapeDtypeStruct((b, s, w), BF16)] * 7 + [jax.ShapeDtypeStruct((b, s, w), F32)] * 2
                 + [jax.ShapeDtypeStruct((b, s // SCAN_TILE, nc, w), F32)])
    out_specs = [tok(w)] * 9 + [pl.BlockSpec((1, nt, nc, w), lambda bi, i: (bi, i, 0, 0))]
    return in_specs, out_shape, out_specs


def _rwkv_scan_body(at_ref, rt_ref, bt_ref, kt_ref, bh_ref, kh_ref, v_ref, g_ref, bonus_ref, ptot_ref,
                    gain_ref, bias_ref, o_ref, state_ref):
    n = B_HEAD
    c_len = CHUNK
    pw = 2 * n
    ts = at_ref.shape[1]

    @pl.when(pl.program_id(1) == 0)
    def _():
        state_ref[...] = jnp.zeros_like(state_ref)

    row = lax.broadcasted_iota(jnp.int32, (c_len, pw), 0)
    lane = lax.broadcasted_iota(jnp.int32, (c_len, pw), 1)
    col = lane & (n - 1)
    strict = col < row
    incl = col <= row
    eye2 = (col == row).astype(F32)
    same_block = lambda bits: (row >> bits) == (col >> bits)
    diag_blocks = same_block(INV_BASE_BITS)
    merge_masks = [same_block(bits + 1) & ~same_block(bits) for bits in range(INV_BASE_BITS, CHUNK_SHIFT)]
    first = lane < n
    half_f = (first.astype(F32), 1.0 - first.astype(F32))
    half_b = tuple(hm[:1].astype(BF16) for hm in half_f)
    sq_r = lax.broadcasted_iota(jnp.int32, (pw, pw), 0)
    sq_c = lax.broadcasted_iota(jnp.int32, (pw, pw), 1)
    pair_blocks = (sq_r >> CHUNK_SHIFT) == (sq_c >> CHUNK_SHIFT)
    pair_eye = sq_r == sq_c

    def bd(y):
        return jnp.concatenate([y * half_b[0], y * half_b[1]], axis=0)

    def mm(x, y_bd):
        return _dot(x.astype(BF16), y_bd)

    n_chunks = ts // c_len
    pairs = range(at_ref.shape[2] // pw)
    units = [(c, q) for c in range(n_chunks) for q in pairs]
    rows = lambda c: slice(c * c_len, (c + 1) * c_len)
    lanes = lambda q: slice(q * pw, (q + 1) * pw)
    load = lambda ref: {(c, q): ref[0, rows(c), lanes(q)] for c, q in units}
    at, rt, bt, kt, bh, kh, vv = (load(r) for r in (at_ref, rt_ref, bt_ref, kt_ref, bh_ref, kh_ref, v_ref))
    state = [state_ref[q] for q in pairs]

    ar = {u: jnp.concatenate([at[u], rt[u]], axis=0) for u in units}
    g_b = {u: _dot_nt(ar[u], bd(bt[u])) for u in units}
    g_k = {u: _dot_nt(ar[u], bd(kt[u])) for u in units}
    l_mat = {u: jnp.where(strict, g_b[u][:c_len], 0.0) for u in units}
    a_ak = {u: jnp.where(strict, g_k[u][:c_len], 0.0).astype(BF16) for u in units}
    a_rb = {u: jnp.where(incl, g_b[u][c_len:], 0.0).astype(BF16) for u in units}
    a_rk = {u: jnp.where(incl, g_k[u][c_len:], 0.0).astype(BF16) for u in units}

    p = {u: jnp.where(diag_blocks, l_mat[u], 0.0) for u in units}
    t_inv = {u: eye2 + p[u] for u in units}
    p = {u: p[u].astype(BF16) for u in units}
    for _ in range(INV_BASE_BITS - 1):
        p = {u: mm(p[u], bd(p[u])).astype(BF16) for u in units}
        t_inv = {u: t_inv[u] + mm(t_inv[u], bd(p[u])) for u in units}
    for off_mask in merge_masks:
        t_bd = {u: bd(t_inv[u].astype(BF16)) for u in units}
        t_l = {u: mm(t_inv[u], bd(jnp.where(off_mask, l_mat[u], 0.0).astype(BF16))) for u in units}
        t_inv = {u: t_inv[u] + mm(t_l[u], t_bd[u]) for u in units}

    v_bd = {u: bd(vv[u]) for u in units}
    akv = {u: mm(a_ak[u], v_bd[u]).astype(BF16) for u in units}
    ap = {u: mm(t_inv[u], bd(at[u])).astype(BF16) for u in units}
    ut = {u: mm(t_inv[u], bd(akv[u])).astype(BF16) for u in units}
    ap_bd = {u: bd(ap[u]) for u in units}
    rp = {u: (rt[u].astype(F32) + mm(a_rb[u], ap_bd[u])).astype(BF16) for u in units}
    y_loc = {u: mm(a_rb[u], bd(ut[u])) + mm(a_rk[u], v_bd[u]) for u in units}
    w_m = {u: _dot_tn(ap[u], bh[u]) for u in units}
    w_g = {u: _dot_tn(jnp.concatenate([ut[u], vv[u]], axis=0), jnp.concatenate([bh[u], kh[u]], axis=0))
           for u in units}
    m_bd = {(c, q): (jnp.where(pair_blocks, w_m[(c, q)], 0.0)
                     + jnp.where(pair_eye, ptot_ref[0, 0, c:c + 1, lanes(q)], 0.0)).astype(BF16)
            for c, q in units}
    g_c = {u: w_g[u][:n] * half_f[0] + w_g[u][n:] * half_f[1] for u in units}

    y = {}
    for c, q in units:
        s0 = state[q].astype(BF16)
        y[(c, q)] = _dot_nt(rp[(c, q)], bd(s0)) + y_loc[(c, q)]
        state[q] = _dot(s0, m_bd[(c, q)]) + g_c[(c, q)]

    for c, q in units:
        yv = y[(c, q)]
        head_mean = lambda z: jnp.where(first, jnp.sum(z * half_f[0], axis=-1, keepdims=True),
                                        jnp.sum(z * half_f[1], axis=-1, keepdims=True)) * (1.0 / n)
        yc = yv - head_mean(yv)
        var = head_mean(yc * yc)
        yn = yc * lax.rsqrt(var + HEAD_NORM_EPS) * gain_ref[:, lanes(q)] + bias_ref[:, lanes(q)]
        o_ref[0, rows(c), lanes(q)] = ((yn + bonus_ref[0, rows(c), lanes(q)])
                                       * g_ref[0, rows(c), lanes(q)]).astype(o_ref.dtype)
    for q in pairs:
        state_ref[q] = state[q]


def _rwkv_scan(arrs, gain, bias):
    at = arrs[0]
    b, s, w = at.shape
    ts = SCAN_TILE
    nc = ts // CHUNK
    tok = pl.BlockSpec((1, ts, w), lambda bi, i: (bi, i, 0))
    vec = pl.BlockSpec((1, w), lambda bi, i: (0, 0))
    return pl.pallas_call(
        _rwkv_scan_body,
        out_shape=jax.ShapeDtypeStruct((b, s, w), BF16),
        grid=(b, s // ts),
        in_specs=[tok] * 9 + [pl.BlockSpec((1, 1, nc, w), lambda bi, i: (bi, i, 0, 0)), vec, vec],
        out_specs=tok,
        scratch_shapes=[pltpu.VMEM((w // (2 * B_HEAD), B_HEAD, 2 * B_HEAD), F32)],
        compiler_params=_cparams(("parallel", "arbitrary"), 32),
        name="rwkv_scan",
    )(*arrs, gain, bias)


def _conv_body(pc_ref, prev_ref, w_ref, g_ref, o_ref):
    w = C_WIDTH
    x = pc_ref[0]
    xp = jnp.where(pl.program_id(1) == 0, 0.0, prev_ref[0])
    u = x[:, w:2 * w] * x[:, 2 * w:3 * w]
    up = xp[:, w:2 * w] * xp[:, 2 * w:3 * w]
    cw = w_ref[...]
    y = cw[0:1] * _shift_rows(up, u, 2) + cw[1:2] * _shift_rows(up, u, 1) + cw[2:3] * u
    z = x[:, 0:w] * y
    ms = _dot_exact_rhs(z * z, _group_ones2(w, w // C_GROUPS)) * (C_GROUPS / w)
    o_ref[0] = (z * lax.rsqrt(ms + RMS_EPS) * g_ref[...]).astype(o_ref.dtype)


def _conv_specs(b, s, ts, conv_w, gain):
    in_specs = [pl.BlockSpec((1, ts, PC_W), lambda bi, i: (bi, i, 0)),
                pl.BlockSpec((1, SUBLANES, PC_W),
                             lambda bi, i: (bi, jnp.maximum(i * (ts // SUBLANES) - 1, 0), 0)),
                pl.BlockSpec(conv_w.shape, lambda bi, i: (0, 0)),
                pl.BlockSpec(gain.shape, lambda bi, i: (0, 0))]
    out_shape = [jax.ShapeDtypeStruct((b, s, C_WIDTH), BF16)]
    return in_specs, out_shape, [pl.BlockSpec((1, ts, C_WIDTH), lambda bi, i: (bi, i, 0))]


def _mixer_prep_body(*refs, n_in, n_out):
    ins, outs = refs[:sum(n_in)], refs[sum(n_in):]
    bodies = (_mla_prep_body, _rwkv_prep_body, _conv_body)
    i0 = o0 = 0
    for body, ni, no in zip(bodies, n_in, n_out):
        body(*ins[i0:i0 + ni], *outs[o0:o0 + no])
        i0 += ni
        o0 += no


def _mixer_prep(cq, ckv, tab, mla_params, pbm, pbl, rw_params, pc, conv_w, conv_gain):
    b, s, _ = cq.shape
    ts = PREP_TILE
    parts = (_mla_prep_specs(b, s, ts, *mla_params), _rwkv_prep_specs(b, s, ts, rw_params),
             _conv_specs(b, s, ts, conv_w, conv_gain))
    outs = pl.pallas_call(
        functools.partial(_mixer_prep_body, n_in=tuple(len(p[0]) for p in parts),
                          n_out=tuple(len(p[1]) for p in parts)),
        out_shape=[sh for p in parts for sh in p[1]],
        grid=(b, s // ts),
        in_specs=[sp for p in parts for sp in p[0]],
        out_specs=[sp for p in parts for sp in p[2]],
        compiler_params=_cparams(("parallel", "parallel"), 60),
        name="mixer_prep",
    )(cq, ckv, tab, *mla_params, pbm, pbl, pbm, pbl, *rw_params, pc, pc, conv_w, conv_gain)
    return outs[:3], outs[3:13], outs[13]


def _outproj_body(h_ref, ya_ref, yb_ref, yc_ref, w_ref, o_ref):
    acc = h_ref[...] + _dot(ya_ref[...], w_ref[0:A_WIDTH, :])
    acc = acc + _dot(yb_ref[...], w_ref[A_WIDTH:A_WIDTH + B_WIDTH, :])
    o_ref[...] = acc + _dot(yc_ref[...], w_ref[A_WIDTH + B_WIDTH:, :])


def _out_proj(h, ya, yb, yc, w, layer):
    t, d = h.shape
    tm = PROJ_TM
    row = lambda n: pl.BlockSpec((tm, n), lambda i: (i, 0))
    return pl.pallas_call(
        _outproj_body,
        out_shape=jax.ShapeDtypeStruct((t, d), F32),
        grid=(t // tm,),
        in_specs=[row(d), row(A_WIDTH), row(B_WIDTH), row(C_WIDTH),
                  pl.BlockSpec((None,) + w.shape[1:], lambda i: (layer, 0, 0))],
        out_specs=row(d),
        compiler_params=_cparams(("parallel",), 48),
        name="out_proj",
    )(h, ya, yb, yc, w)


def _swap_halves(w):
    half = ROPE_DIM // 2
    return jnp.concatenate([w[..., half:], w[..., :half]], axis=-1)


def _arrange_w_in(w_in):
    w_in = w_in.astype(BF16)
    o = Q_LORA + KV_LORA
    k_rope = w_in[:, o:A_IN]
    b0 = A_IN
    return jnp.concatenate(
        [w_in[:, :Q_LORA], w_in[:, Q_LORA:o], k_rope, _swap_halves(k_rope),
         w_in[:, b0:b0 + PBM_W], _spread_lora(w_in[:, b0 + PBM_W:b0 + B_IN]), w_in[:, b0 + B_IN:]], axis=1)


def _spread_lora(cols):
    n_wa = DECAY_LORA + ICLR_LORA
    zeros = lambda n: jnp.zeros(cols.shape[:-1] + (n,), cols.dtype)
    return jnp.concatenate([cols[..., :n_wa], zeros(LANES - n_wa), cols[..., n_wa:], zeros(LANES - GATE_LORA)],
                           axis=-1)


def _arrange_w_uq(w_uq):
    w = w_uq.astype(BF16).reshape(Q_LORA, A_HEADS, QK_DIM)
    nope = w[:, :, :NOPE_DIM].reshape(Q_LORA, -1)
    rope = w[:, :, NOPE_DIM:]
    return jnp.concatenate(
        [nope, rope.reshape(Q_LORA, -1), _swap_halves(rope).reshape(Q_LORA, -1)], axis=1)


def _arrange_w_ukv(w_ukv):
    w = w_ukv.astype(BF16).reshape(KV_LORA, A_HEADS, NOPE_DIM + V_DIM)
    return jnp.concatenate(
        [w[:, :, :NOPE_DIM].reshape(KV_LORA, -1), w[:, :, NOPE_DIM:].reshape(KV_LORA, -1)], axis=1)


def _pad_rows(w, start):
    full = jnp.zeros((LANES, w.shape[1]), F32).at[start:start + w.shape[0]].set(w)
    hi = full.astype(BF16)
    return jnp.concatenate([hi, hi, (full - hi.astype(F32)).astype(BF16)], axis=0)


def _row(v):
    return v.reshape(1, -1)


def kernel(x, positions, norm_ffn1, ffn1_gate, ffn1_up, ffn1_down, norm_mix, w_in, q_norm, kv_norm, w_uq, w_ukv, attn_out_norm, shift_mu, decay_w0, decay_up, iclr_a0, iclr_up, gate_up, k_k, k_a, r_k, lnx_gain, lnx_bias, conv_w, conv_out_norm, w_out, norm_ffn2, ffn2_gate, ffn2_up, ffn2_down, norm_final):
    b, s, d = x.shape
    t = b * s
    seq = lambda a: a.reshape(b, s, a.shape[-1])

    inv_freq = 1.0 / (ROPE_THETA ** (jnp.arange(0, ROPE_DIM, 2, dtype=F32) / ROPE_DIM))
    inv_lane = jnp.tile(inv_freq, LANES // inv_freq.shape[0]).reshape(1, LANES)
    tab = seq(_rope_table(positions.reshape(t, 1), inv_lane))

    h = x.reshape(t, d)
    g_final = _row(norm_final)
    ffn1 = tuple(w.astype(BF16) for w in (ffn1_gate, ffn1_up, ffn1_down))
    ffn2 = tuple(w.astype(BF16) for w in (ffn2_gate, ffn2_up, ffn2_down))
    w_out_b = w_out.astype(BF16)
    for l in range(DEPTH):
        h = _ffn(h, _row(norm_ffn1[l]), *ffn1, l, g_final, False)

        cq, ckv, pbm, pbl, pc = _in_proj(h, _row(norm_mix[l]), _arrange_w_in(w_in[l]))

        mu = shift_mu[l]
        mu_l = _spread_lora(mu[PBM_W:])
        mla_params = (_row(q_norm[l]), _row(kv_norm[l]), _arrange_w_uq(w_uq[l]), _arrange_w_ukv(w_ukv[l]))
        rw_params = (_row(mu[:PBM_W]), _row(mu_l), _row(decay_w0[l]), _row(iclr_a0[l]),
                     _pad_rows(decay_up[l], 0), _pad_rows(iclr_up[l], DECAY_LORA), _pad_rows(gate_up[l], 0),
                     _row(k_k[l]), _row(k_a[l]), _row(r_k[l]))
        (q, k, v), rw, yc = _mixer_prep(seq(cq), seq(ckv), tab, mla_params, seq(pbm), seq(pbl), rw_params,
                                        seq(pc), conv_w[l], _row(conv_out_norm[l]))
        ya = _attention(q, k, v, _row(attn_out_norm[l]))
        yb = _rwkv_scan(rw, _row(lnx_gain[l]), _row(lnx_bias[l]))

        h = _out_proj(h, ya.reshape(t, A_WIDTH), yb.reshape(t, B_WIDTH), yc.reshape(t, C_WIDTH), w_out_b, l)

        h = _ffn(h, _row(norm_ffn2[l]), *ffn2, l, g_final, l == DEPTH - 1)
    return h.reshape(b, s, d)
```

```python
import functools

import jax
import jax.numpy as jnp
from jax import lax
from jax.experimental import pallas as pl
from jax.experimental.pallas import tpu as pltpu

F32 = jnp.float32
BF16 = jnp.bfloat16

D_MODEL = 2048
DEPTH = 4
A_WIDTH = 1024
B_WIDTH = 512
C_WIDTH = 512
V_DIM = 128
NOPE_DIM = 128
ROPE_DIM = 64
QK_DIM = NOPE_DIM + ROPE_DIM
QK_PAD = 256
A_HEADS = 8
Q_LORA = 512
KV_LORA = 256
ROPE_THETA = 10000.0
B_HEAD = 64
B_HEADS = 8
DECAY_LORA = 32
ICLR_LORA = 32
GATE_LORA = 96
HEAD_NORM_EPS = 64e-5
C_GROUPS = 8
CONV_K = 3
D_FF = 5632
RMS_EPS = 1e-6
LOG2_E = 1.4426950408889634
DECAY_LOG_BOUND = 0.6065306597126334
A_IN = Q_LORA + KV_LORA + ROPE_DIM
B_IN = 3 * B_WIDTH + DECAY_LORA + ICLR_LORA + GATE_LORA

CKV_W = KV_LORA + 2 * ROPE_DIM
PBM_W = 3 * B_WIDTH
PBL_W = 256
PC_W = 3 * C_WIDTH
P_SPLITS = (Q_LORA, CKV_W, PBM_W, PBL_W, PC_W)
P_TOTAL = sum(P_SPLITS)

V7X_VMEM_BYTES = 64 * 1024 * 1024
SUBLANES = 8
LANES = 128
FFN_TM = 1024
FFN_TF = 512
PROJ_TM = 512
SCAN_TILE = 512
PREP_TILE = 512
CUM_TILE = 256
ATTN_TQ = 256
CHUNK = 64
CHUNK_SHIFT = CHUNK.bit_length() - 1
INV_BASE_BITS = 3
ROPE_TW = 3 * LANES


def _cparams(sem, vmem_mb):
    return pltpu.CompilerParams(dimension_semantics=sem, vmem_limit_bytes=vmem_mb * 1024 * 1024)


def _rms(x, g, eps=RMS_EPS):
    return x * lax.rsqrt(jnp.mean(x * x, axis=-1, keepdims=True) + eps) * g


def _dot(a, b):
    return jnp.dot(a, b, preferred_element_type=F32)


def _dot_nt(a, b):
    return lax.dot_general(a, b, (((1,), (1,)), ((), ())), preferred_element_type=F32)


def _dot_tn(a, b):
    return lax.dot_general(a, b, (((0,), (0,)), ((), ())), preferred_element_type=F32)


def _split(x):
    hi = x.astype(BF16)
    lo = (x - hi.astype(F32)).astype(BF16)
    return hi, lo


def _dot_exact_rhs(x, m2):
    hi, lo = _split(x)
    return _dot(jnp.concatenate([hi, lo], axis=1), m2)


def _dot3(x, w3):
    hi, lo = _split(x)
    return _dot(jnp.concatenate([hi, lo, hi], axis=1), w3)


def _shift_rows(prev8, x, k):
    cat = jnp.concatenate([prev8, x], axis=0)
    return pltpu.roll(cat, k, axis=0)[SUBLANES:]


def _group_ones2(n, group):
    shift = group.bit_length() - 1
    r = (lax.broadcasted_iota(jnp.int32, (2 * n, n), 0) & (n - 1)) >> shift
    c = lax.broadcasted_iota(jnp.int32, (2 * n, n), 1) >> shift
    return (r == c).astype(BF16)


def _rope_body(pos_ref, inv_ref, o_ref):
    ang = pos_ref[...].astype(F32) * inv_ref[...]
    c = jnp.cos(ang)
    s = jnp.sin(ang)
    lane = lax.broadcasted_iota(jnp.int32, ang.shape, 1)
    ssg = jnp.where((lane & (ROPE_DIM - 1)) < ROPE_DIM // 2, -s, s)
    o_ref[:, 0:LANES] = c
    o_ref[:, LANES:2 * LANES] = ssg
    o_ref[:, 2 * LANES:3 * LANES] = jnp.where(lane < ROPE_DIM, c, ssg)


def _rope_table(pos_col, inv_lane):
    t = pos_col.shape[0]
    tm = 1024
    return pl.pallas_call(
        _rope_body,
        out_shape=jax.ShapeDtypeStruct((t, ROPE_TW), F32),
        grid=(t // tm,),
        in_specs=[pl.BlockSpec((tm, 1), lambda i: (i, 0)),
                  pl.BlockSpec((1, LANES), lambda i: (0, 0))],
        out_specs=pl.BlockSpec((tm, ROPE_TW), lambda i: (i, 0)),
        compiler_params=_cparams(("parallel",), 16),
        name="rope_table",
    )(pos_col, inv_lane)


def _ffn_body(x_ref, g_ref, wg_ref, wu_ref, wd_ref, gf_ref, o_ref, xn_ref, *, final):
    j = pl.program_id(1)

    @pl.when(j == 0)
    def _():
        x = x_ref[...]
        xn_ref[...] = _rms(x, g_ref[...]).astype(BF16)
        o_ref[...] = x

    xn = xn_ref[...]
    gate = _dot(xn, wg_ref[...])
    up = _dot(xn, wu_ref[...])
    act = (gate * jax.nn.sigmoid(gate)) * (up * 0.5)
    o_ref[...] += _dot(act.astype(BF16), wd_ref[...])

    if final:
        @pl.when(j == pl.num_programs(1) - 1)
        def _():
            o_ref[...] = _rms(o_ref[...], gf_ref[...])


def _ffn(h, g, wg, wu, wd, layer, g_final, final):
    t, d = h.shape
    f = wg.shape[2]
    tm, tf = FFN_TM, FFN_TF
    return pl.pallas_call(
        functools.partial(_ffn_body, final=final),
        out_shape=jax.ShapeDtypeStruct((t, d), F32),
        grid=(t // tm, f // tf),
        in_specs=[pl.BlockSpec((tm, d), lambda i, j: (i, 0)),
                  pl.BlockSpec((1, d), lambda i, j: (0, 0)),
                  pl.BlockSpec((None, d, tf), lambda i, j: (layer, 0, j)),
                  pl.BlockSpec((None, d, tf), lambda i, j: (layer, 0, j)),
                  pl.BlockSpec((None, tf, d), lambda i, j: (layer, j, 0)),
                  pl.BlockSpec((1, d), lambda i, j: (0, 0))],
        out_specs=pl.BlockSpec((tm, d), lambda i, j: (i, 0)),
        scratch_shapes=[pltpu.VMEM((tm, d), BF16)],
        compiler_params=_cparams(("parallel", "arbitrary"), 60),
        name="ffn",
    )(h, g, wg, wu, wd, g_final)


def _inproj_body(x_ref, g_ref, w_ref, cq_ref, ckv_ref, pbm_ref, pbl_ref, pc_ref):
    xn = _rms(x_ref[...], g_ref[...]).astype(BF16)
    off = 0
    for ref in (cq_ref, ckv_ref, pbm_ref, pbl_ref, pc_ref):
        n = ref.shape[-1]
        ref[...] = _dot(xn, w_ref[:, off:off + n])
        off += n


def _in_proj(h, g, w):
    t, d = h.shape
    tm = PROJ_TM
    return pl.pallas_call(
        _inproj_body,
        out_shape=[jax.ShapeDtypeStruct((t, n), F32) for n in P_SPLITS],
        grid=(t // tm,),
        in_specs=[pl.BlockSpec((tm, d), lambda i: (i, 0)),
                  pl.BlockSpec((1, d), lambda i: (0, 0)),
                  pl.BlockSpec((d, P_TOTAL), lambda i: (0, 0), pipeline_mode=pl.Buffered(1))],
        out_specs=[pl.BlockSpec((tm, n), lambda i: (i, 0)) for n in P_SPLITS],
        compiler_params=_cparams(("parallel",), 56),
        name="in_proj",
    )(h, g, w)


def _mla_prep_body(cq_ref, ckv_ref, tab_ref, qn_ref, kvn_ref, wq_ref, wkv_ref, q_ref, k_ref, v_ref):
    nq = A_HEADS * NOPE_DIM
    nr = A_HEADS * ROPE_DIM
    scale = QK_DIM ** -0.5 * LOG2_E
    tab = tab_ref[0]
    cos2 = tab[:, 0:LANES]
    sin2 = tab[:, LANES:2 * LANES]
    cs = tab[:, 2 * LANES:3 * LANES]

    cqn = _rms(cq_ref[0], qn_ref[...]).astype(BF16)
    qa = _dot(cqn, wq_ref[...])
    cos_h = jnp.concatenate([cos2] * (nr // LANES), axis=1)
    sin_h = jnp.concatenate([sin2] * (nr // LANES), axis=1)
    q_rot = qa[:, nq:nq + nr] * cos_h + qa[:, nq + nr:nq + 2 * nr] * sin_h

    ckv = ckv_ref[0]
    ckvn = _rms(ckv[:, :KV_LORA], kvn_ref[...]).astype(BF16)
    kv = _dot(ckvn, wkv_ref[...])
    t = ckv[:, KV_LORA:KV_LORA + LANES] * cs
    low_half = lax.broadcasted_iota(jnp.int32, t.shape, 1) < ROPE_DIM
    k_rot = jnp.where(low_half, t + pltpu.roll(t, ROPE_DIM, axis=1), 0.0).astype(BF16)
    for h in range(A_HEADS):
        pair = q_rot[:, (h // 2) * LANES:(h // 2 + 1) * LANES]
        if h % 2:
            pair = pltpu.roll(pair, ROPE_DIM, axis=1)
        q_rope = jnp.where(low_half, pair, 0.0)
        q_nope = qa[:, h * NOPE_DIM:(h + 1) * NOPE_DIM]
        q_ref[0, h] = (jnp.concatenate([q_nope, q_rope], axis=1) * scale).astype(BF16)
        k_ref[0, h] = jnp.concatenate([kv[:, h * NOPE_DIM:(h + 1) * NOPE_DIM].astype(BF16), k_rot], axis=1)
        v_ref[0, h] = kv[:, nq + h * V_DIM:nq + (h + 1) * V_DIM].astype(BF16)


def _mla_prep_specs(b, s, ts, qn, kvn, wq, wkv):
    tok = lambda n: pl.BlockSpec((1, ts, n), lambda bi, i: (bi, i, 0))
    full = lambda a: pl.BlockSpec(a.shape, lambda bi, i: (0, 0))
    head = lambda n: pl.BlockSpec((1, A_HEADS, ts, n), lambda bi, i: (bi, 0, i, 0))
    in_specs = [tok(Q_LORA), tok(CKV_W), tok(ROPE_TW), full(qn), full(kvn), full(wq), full(wkv)]
    out_shape = [jax.ShapeDtypeStruct((b, A_HEADS, s, QK_PAD), BF16),
                 jax.ShapeDtypeStruct((b, A_HEADS, s, QK_PAD), BF16),
                 jax.ShapeDtypeStruct((b, A_HEADS, s, V_DIM), BF16)]
    return in_specs, out_shape, [head(QK_PAD), head(QK_PAD), head(V_DIM)]


def _attn_body(q_ref, k_ref, v_ref, g_ref, o_ref, s_ref, mx_ref, m_ref, acc_ref):
    i = pl.program_id(1)
    nh = q_ref.shape[1]
    tq = q_ref.shape[2]
    heads = range(nh)
    ones = jnp.ones((tq, V_DIM), BF16)
    keys = lambda j: pl.ds(pl.multiple_of(j * tq, tq), tq)

    row = lax.broadcasted_iota(jnp.int32, (tq, tq), 0)
    col = lax.broadcasted_iota(jnp.int32, (tq, tq), 1)
    for h in heads:
        s = jnp.where(col <= row, _dot_nt(q_ref[0, h], k_ref[0, h, keys(i), :]), -1e30)
        s_ref[h, i] = s
        mx_ref[h] = s

    def scores(blocks):
        for h in heads:
            s = [_dot_nt(q_ref[0, h], k_ref[0, h, keys(j), :]) for j in blocks]
            for j, s_j in zip(blocks, s):
                s_ref[h, j] = s_j
            mx_ref[h] = jnp.maximum(mx_ref[h], functools.reduce(jnp.maximum, s))

    def score_pair(jj, _):
        scores((2 * jj, 2 * jj + 1))
        return 0

    lax.fori_loop(0, i >> 1, score_pair, 0)

    @pl.when((i & 1) == 1)
    def _():
        scores((i - 1,))

    for h in heads:
        m_ref[h] = jnp.broadcast_to(jnp.max(mx_ref[h], axis=-1, keepdims=True), (tq, LANES))

    def weighted(j, h):
        m = m_ref[h]
        p = jnp.exp2(s_ref[h, j] - jnp.concatenate([m] * (tq // LANES), axis=1))
        return _dot(p.astype(BF16), jnp.concatenate([v_ref[0, h, keys(j), :], ones], axis=1))

    for h in heads:
        acc_ref[h] = weighted(i, h)

    def values(blocks):
        for h in heads:
            acc_ref[h] += functools.reduce(lambda a, b: a + b, [weighted(j, h) for j in blocks])

    def value_pair(jj, _):
        values((2 * jj, 2 * jj + 1))
        return 0

    lax.fori_loop(0, i >> 1, value_pair, 0)

    @pl.when((i & 1) == 1)
    def _():
        values((i - 1,))

    for h in heads:
        acc = acc_ref[h]
        hs = slice(h * V_DIM, (h + 1) * V_DIM)
        o_ref[0, :, hs] = _rms(acc[:, :V_DIM] / acc[:, V_DIM:], g_ref[:, hs]).astype(o_ref.dtype)


def _attention(q, k, v, g):
    b, nh, s, _ = q.shape
    tq = ATTN_TQ
    return pl.pallas_call(
        _attn_body,
        out_shape=jax.ShapeDtypeStruct((b, s, nh * V_DIM), BF16),
        grid=(b, s // tq),
        in_specs=[pl.BlockSpec((1, nh, tq, QK_PAD), lambda bi, i: (bi, 0, i, 0)),
                  pl.BlockSpec((1, nh, s, QK_PAD), lambda bi, i: (bi, 0, 0, 0)),
                  pl.BlockSpec((1, nh, s, V_DIM), lambda bi, i: (bi, 0, 0, 0)),
                  pl.BlockSpec((1, nh * V_DIM), lambda bi, i: (0, 0))],
        out_specs=pl.BlockSpec((1, tq, nh * V_DIM), lambda bi, i: (bi, i, 0)),
        scratch_shapes=[pltpu.VMEM((nh, s // tq, tq, tq), F32),
                        pltpu.VMEM((nh, tq, tq), F32),
                        pltpu.VMEM((nh, tq, LANES), F32),
                        pltpu.VMEM((nh, tq, 2 * V_DIM), F32)],
        compiler_params=_cparams(("parallel", "arbitrary"), 56),
        name="mla_attention",
    )(q, k, v, g)


def _rwkv_prep_body(pbm_ref, pbl_ref, pbm_prev_ref, pbl_prev_ref, mum_ref, mul_ref, w0_ref, a0_ref,
                    dec_ref, icl_ref, gat_ref, kk_ref, ka_ref, rk_ref,
                    at_ref, rt_ref, bt_ref, kt_ref, bh_ref, kh_ref, v_ref, g_ref, bonus_ref, ptot_ref):
    i = pl.program_id(1)
    ts = pbm_ref.shape[1]
    w = B_WIDTH
    first = i == 0

    xm = pbm_ref[0]
    xl = pbl_ref[0]
    pm = jnp.where(first, 0.0, pbm_prev_ref[0])
    pv = jnp.where(first, 0.0, pbl_prev_ref[0])
    xs = xm + (_shift_rows(pm, xm, 1) - xm) * mum_ref[...]
    ls = xl + (_shift_rows(pv, xl, 1) - xl) * mul_ref[...]
    r = xs[:, 0:w]
    k = xs[:, w:2 * w]
    v = xs[:, 2 * w:3 * w]

    ls_wa = ls[:, :LANES]
    wl = w0_ref[...] + _dot3(jnp.tanh(ls_wa), dec_ref[...])
    lw = (-DECAY_LOG_BOUND * LOG2_E) * jax.nn.sigmoid(wl)
    a_ic = jax.nn.sigmoid(a0_ref[...] + _dot3(ls_wa, icl_ref[...]))
    gate = _dot3(jax.nn.sigmoid(ls[:, LANES:]), gat_ref[...])

    seg2 = _group_ones2(w, B_HEAD)
    kk = k * kk_ref[...]
    kkn = kk * jnp.minimum(lax.rsqrt(_dot_exact_rhs(kk * kk, seg2)), 1e12)
    kp = k * (1.0 + (a_ic - 1.0) * ka_ref[...])
    bonus = _dot_exact_rhs(r * kp * rk_ref[...], seg2) * v

    ct = CUM_TILE
    row = lax.broadcasted_iota(jnp.int32, (2 * ct, 2 * ct), 0)
    col = lax.broadcasted_iota(jnp.int32, (2 * ct, 2 * ct), 1) & (ct - 1)
    same = ((row & (ct - 1)) >> CHUNK_SHIFT) == (col >> CHUNK_SHIFT)
    pos = row & (ct - 1)
    first_col = jnp.where(row < ct, 0, pos + 1)
    last_col = jnp.where(row < ct, pos, ct)
    tri2 = (same & (col >= first_col) & (col <= last_col)).astype(BF16)
    lw_hi, lw_lo = _split(lw)
    cums = [_dot(tri2, jnp.concatenate([lw_hi[t0:t0 + ct], lw_lo[t0:t0 + ct]], axis=0))
            for t0 in range(0, ts, ct)]
    cum = jnp.concatenate([c[:ct] for c in cums], axis=0)
    cumr = jnp.concatenate([c[ct:] for c in cums], axis=0)

    e_neg = jnp.exp2(-cum)
    e_rem = jnp.exp2(cumr)
    b_vec = kkn * a_ic
    at_ref[0] = (-kkn * jnp.exp2(cum - lw)).astype(BF16)
    rt_ref[0] = (r * jnp.exp2(cum)).astype(BF16)
    bt_ref[0] = (b_vec * e_neg).astype(BF16)
    kt_ref[0] = (kp * e_neg).astype(BF16)
    bh_ref[0] = (b_vec * e_rem).astype(BF16)
    kh_ref[0] = (kp * e_rem).astype(BF16)
    v_ref[0] = v.astype(BF16)
    g_ref[0] = gate
    bonus_ref[0] = bonus
    nc = ptot_ref.shape[2]
    for t in range(ptot_ref.shape[1]):
        last_rows = [(t * nc + c + 1) * CHUNK - 1 for c in range(nc)]
        ptot_ref[0, t] = jnp.exp2(jnp.concatenate([cum[r0:r0 + 1] for r0 in last_rows], axis=0))


def _rwkv_prep_specs(b, s, ts, params):
    nc = SCAN_TILE // CHUNK
    nt = ts // SCAN_TILE
    w = B_WIDTH
    tok = lambda n: pl.BlockSpec((1, ts, n), lambda bi, i: (bi, i, 0))
    prev = lambda n: pl.BlockSpec(
        (1, SUBLANES, n), lambda bi, i: (bi, jnp.maximum(i * (ts // SUBLANES) - 1, 0), 0))
    full = lambda a: pl.BlockSpec(a.shape, lambda bi, i: (0, 0))
    in_specs = [tok(PBM_W), tok(PBL_W), prev(PBM_W), prev(PBL_W)] + [full(p) for p in params]
    out_shape = ([jax.ShapeDtypeStruct((b, s, w), BF16)] * 7 + [jax.ShapeDtypeStruct((b, s, w), F32)] * 2
                 + [jax.ShapeDtypeStruct((b, s // SCAN_TILE, nc, w), F32)])
    out_specs = [tok(w)] * 9 + [pl.BlockSpec((1, nt, nc, w), lambda bi, i: (bi, i, 0, 0))]
    return in_specs, out_shape, out_specs


def _rwkv_scan_body(at_ref, rt_ref, bt_ref, kt_ref, bh_ref, kh_ref, v_ref, g_ref, bonus_ref, ptot_ref,
                    gain_ref, bias_ref, o_ref, state_ref):
    n = B_HEAD
    c_len = CHUNK
    pw = 2 * n
    ts = at_ref.shape[1]

    @pl.when(pl.program_id(1) == 0)
    def _():
        state_ref[...] = jnp.zeros_like(state_ref)

    row = lax.broadcasted_iota(jnp.int32, (c_len, pw), 0)
    lane = lax.broadcasted_iota(jnp.int32, (c_len, pw), 1)
    col = lane & (n - 1)
    strict = col < row
    incl = col <= row
    eye2 = (col == row).astype(F32)
    same_block = lambda bits: (row >> bits) == (col >> bits)
    diag_blocks = same_block(INV_BASE_BITS)
    merge_masks = [same_block(bits + 1) & ~same_block(bits) for bits in range(INV_BASE_BITS, CHUNK_SHIFT)]
    first = lane < n
    half_f = (first.astype(F32), 1.0 - first.astype(F32))
    half_b = tuple(hm[:1].astype(BF16) for hm in half_f)
    sq_r = lax.broadcasted_iota(jnp.int32, (pw, pw), 0)
    sq_c = lax.broadcasted_iota(jnp.int32, (pw, pw), 1)
    pair_blocks = (sq_r >> CHUNK_SHIFT) == (sq_c >> CHUNK_SHIFT)
    pair_eye = sq_r == sq_c

    def bd(y):
        return jnp.concatenate([y * half_b[0], y * half_b[1]], axis=0)

    def mm(x, y_bd):
        return _dot(x.astype(BF16), y_bd)

    n_chunks = ts // c_len
    pairs = range(at_ref.shape[2] // pw)
    units = [(c, q) for c in range(n_chunks) for q in pairs]
    rows = lambda c: slice(c * c_len, (c + 1) * c_len)
    lanes = lambda q: slice(q * pw, (q + 1) * pw)
    load = lambda ref: {(c, q): ref[0, rows(c), lanes(q)] for c, q in units}
    at, rt, bt, kt, bh, kh, vv = (load(r) for r in (at_ref, rt_ref, bt_ref, kt_ref, bh_ref, kh_ref, v_ref))
    state = [state_ref[q] for q in pairs]

    ar = {u: jnp.concatenate([at[u], rt[u]], axis=0) for u in units}
    g_b = {u: _dot_nt(ar[u], bd(bt[u])) for u in units}
    g_k = {u: _dot_nt(ar[u], bd(kt[u])) for u in units}
    l_mat = {u: jnp.where(strict, g_b[u][:c_len], 0.0) for u in units}
    a_ak = {u: jnp.where(strict, g_k[u][:c_len], 0.0).astype(BF16) for u in units}
    a_rb = {u: jnp.where(incl, g_b[u][c_len:], 0.0).astype(BF16) for u in units}
    a_rk = {u: jnp.where(incl, g_k[u][c_len:], 0.0).astype(BF16) for u in units}

    p = {u: jnp.where(diag_blocks, l_mat[u], 0.0) for u in units}
    t_inv = {u: eye2 + p[u] for u in units}
    p = {u: p[u].astype(BF16) for u in units}
    for _ in range(INV_BASE_BITS - 1):
        p = {u: mm(p[u], bd(p[u])).astype(BF16) for u in units}
        t_inv = {u: t_inv[u] + mm(t_inv[u], bd(p[u])) for u in units}
    for off_mask in merge_masks:
        t_bd = {u: bd(t_inv[u].astype(BF16)) for u in units}
        t_l = {u: mm(t_inv[u], bd(jnp.where(off_mask, l_mat[u], 0.0).astype(BF16))) for u in units}
        t_inv = {u: t_inv[u] + mm(t_l[u], t_bd[u]) for u in units}

    v_bd = {u: bd(vv[u]) for u in units}
    akv = {u: mm(a_ak[u], v_bd[u]).astype(BF16) for u in units}
    ap = {u: mm(t_inv[u], bd(at[u])).astype(BF16) for u in units}
    ut = {u: mm(t_inv[u], bd(akv[u])).astype(BF16) for u in units}
    ap_bd = {u: bd(ap[u]) for u in units}
    rp = {u: (rt[u].astype(F32) + mm(a_rb[u], ap_bd[u])).astype(BF16) for u in units}
    y_loc = {u: mm(a_rb[u], bd(ut[u])) + mm(a_rk[u], v_bd[u]) for u in units}
    w_m = {u: _dot_tn(ap[u], bh[u]) for u in units}
    w_g = {u: _dot_tn(jnp.concatenate([ut[u], vv[u]], axis=0), jnp.concatenate([bh[u], kh[u]], axis=0))
           for u in units}
    m_bd = {(c, q): (jnp.where(pair_blocks, w_m[(c, q)], 0.0)
                     + jnp.where(pair_eye, ptot_ref[0, 0, c:c + 1, lanes(q)], 0.0)).astype(BF16)
            for c, q in units}
    g_c = {u: w_g[u][:n] * half_f[0] + w_g[u][n:] * half_f[1] for u in units}

    y = {}
    for c, q in units:
        s0 = state[q].astype(BF16)
        y[(c, q)] = _dot_nt(rp[(c, q)], bd(s0)) + y_loc[(c, q)]
        state[q] = _dot(s0, m_bd[(c, q)]) + g_c[(c, q)]

    for c, q in units:
        yv = y[(c, q)]
        head_mean = lambda z: jnp.where(first, jnp.sum(z * half_f[0], axis=-1, keepdims=True),
                                        jnp.sum(z * half_f[1], axis=-1, keepdims=True)) * (1.0 / n)
        yc = yv - head_mean(yv)
        var = head_mean(yc * yc)
        yn = yc * lax.rsqrt(var + HEAD_NORM_EPS) * gain_ref[:, lanes(q)] + bias_ref[:, lanes(q)]
        o_ref[0, rows(c), lanes(q)] = ((yn + bonus_ref[0, rows(c), lanes(q)])
                                       * g_ref[0, rows(c), lanes(q)]).astype(o_ref.dtype)
    for q in pairs:
        state_ref[q] = state[q]


def _rwkv_scan(arrs, gain, bias):
    at = arrs[0]
    b, s, w = at.shape
    ts = SCAN_TILE
    nc = ts // CHUNK
    tok = pl.BlockSpec((1, ts, w), lambda bi, i: (bi, i, 0))
    vec = pl.BlockSpec((1, w), lambda bi, i: (0, 0))
    return pl.pallas_call(
        _rwkv_scan_body,
        out_shape=jax.ShapeDtypeStruct((b, s, w), BF16),
        grid=(b, s // ts),
        in_specs=[tok] * 9 + [pl.BlockSpec((1, 1, nc, w), lambda bi, i: (bi, i, 0, 0)), vec, vec],
        out_specs=tok,
        scratch_shapes=[pltpu.VMEM((w // (2 * B_HEAD), B_HEAD, 2 * B_HEAD), F32)],
        compiler_params=_cparams(("parallel", "arbitrary"), 32),
        name="rwkv_scan",
    )(*arrs, gain, bias)


def _conv_body(pc_ref, prev_ref, w_ref, g_ref, o_ref):
    w = C_WIDTH
    x = pc_ref[0]
    xp = jnp.where(pl.program_id(1) == 0, 0.0, prev_ref[0])
    u = x[:, w:2 * w] * x[:, 2 * w:3 * w]
    up = xp[:, w:2 * w] * xp[:, 2 * w:3 * w]
    cw = w_ref[...]
    y = cw[0:1] * _shift_rows(up, u, 2) + cw[1:2] * _shift_rows(up, u, 1) + cw[2:3] * u
    z = x[:, 0:w] * y
    ms = _dot_exact_rhs(z * z, _group_ones2(w, w // C_GROUPS)) * (C_GROUPS / w)
    o_ref[0] = (z * lax.rsqrt(ms + RMS_EPS) * g_ref[...]).astype(o_ref.dtype)


def _conv_specs(b, s, ts, conv_w, gain):
    in_specs = [pl.BlockSpec((1, ts, PC_W), lambda bi, i: (bi, i, 0)),
                pl.BlockSpec((1, SUBLANES, PC_W),
                             lambda bi, i: (bi, jnp.maximum(i * (ts // SUBLANES) - 1, 0), 0)),
                pl.BlockSpec(conv_w.shape, lambda bi, i: (0, 0)),
                pl.BlockSpec(gain.shape, lambda bi, i: (0, 0))]
    out_shape = [jax.ShapeDtypeStruct((b, s, C_WIDTH), BF16)]
    return in_specs, out_shape, [pl.BlockSpec((1, ts, C_WIDTH), lambda bi, i: (bi, i, 0))]


def _mixer_prep_body(*refs, n_in, n_out):
    ins, outs = refs[:sum(n_in)], refs[sum(n_in):]
    bodies = (_mla_prep_body, _rwkv_prep_body, _conv_body)
    i0 = o0 = 0
    for body, ni, no in zip(bodies, n_in, n_out):
        body(*ins[i0:i0 + ni], *outs[o0:o0 + no])
        i0 += ni
        o0 += no


def _mixer_prep(cq, ckv, tab, mla_params, pbm, pbl, rw_params, pc, conv_w, conv_gain):
    b, s, _ = cq.shape
    ts = PREP_TILE
    parts = (_mla_prep_specs(b, s, ts, *mla_params), _rwkv_prep_specs(b, s, ts, rw_params),
             _conv_specs(b, s, ts, conv_w, conv_gain))
    outs = pl.pallas_call(
        functools.partial(_mixer_prep_body, n_in=tuple(len(p[0]) for p in parts),
                          n_out=tuple(len(p[1]) for p in parts)),
        out_shape=[sh for p in parts for sh in p[1]],
        grid=(b, s // ts),
        in_specs=[sp for p in parts for sp in p[0]],
        out_specs=[sp for p in parts for sp in p[2]],
        compiler_params=_cparams(("parallel", "parallel"), 60),
        name="mixer_prep",
    )(cq, ckv, tab, *mla_params, pbm, pbl, pbm, pbl, *rw_params, pc, pc, conv_w, conv_gain)
    return outs[:3], outs[3:13], outs[13]


def _outproj_body(h_ref, ya_ref, yb_ref, yc_ref, w_ref, o_ref):
    acc = h_ref[...] + _dot(ya_ref[...], w_ref[0:A_WIDTH, :])
    acc = acc + _dot(yb_ref[...], w_ref[A_WIDTH:A_WIDTH + B_WIDTH, :])
    o_ref[...] = acc + _dot(yc_ref[...], w_ref[A_WIDTH + B_WIDTH:, :])


def _out_proj(h, ya, yb, yc, w, layer):
    t, d = h.shape
    tm = PROJ_TM
    row = lambda n: pl.BlockSpec((tm, n), lambda i: (i, 0))
    return pl.pallas_call(
        _outproj_body,
        out_shape=jax.ShapeDtypeStruct((t, d), F32),
        grid=(t // tm,),
        in_specs=[row(d), row(A_WIDTH), row(B_WIDTH), row(C_WIDTH),
                  pl.BlockSpec((None,) + w.shape[1:], lambda i: (layer, 0, 0))],
        out_specs=row(d),
        compiler_params=_cparams(("parallel",), 48),
        name="out_proj",
    )(h, ya, yb, yc, w)


def _swap_halves(w):
    half = ROPE_DIM // 2
    return jnp.concatenate([w[..., half:], w[..., :half]], axis=-1)


def _arrange_w_in(w_in):
    w_in = w_in.astype(BF16)
    o = Q_LORA + KV_LORA
    k_rope = w_in[:, o:A_IN]
    b0 = A_IN
    return jnp.concatenate(
        [w_in[:, :Q_LORA], w_in[:, Q_LORA:o], k_rope, _swap_halves(k_rope),
         w_in[:, b0:b0 + PBM_W], _spread_lora(w_in[:, b0 + PBM_W:b0 + B_IN]), w_in[:, b0 + B_IN:]], axis=1)


def _spread_lora(cols):
    n_wa = DECAY_LORA + ICLR_LORA
    zeros = lambda n: jnp.zeros(cols.shape[:-1] + (n,), cols.dtype)
    return jnp.concatenate([cols[..., :n_wa], zeros(LANES - n_wa), cols[..., n_wa:], zeros(LANES - GATE_LORA)],
                           axis=-1)


def _arrange_w_uq(w_uq):
    w = w_uq.astype(BF16).reshape(Q_LORA, A_HEADS, QK_DIM)
    nope = w[:, :, :NOPE_DIM].reshape(Q_LORA, -1)
    rope = w[:, :, NOPE_DIM:]
    return jnp.concatenate(
        [nope, rope.reshape(Q_LORA, -1), _swap_halves(rope).reshape(Q_LORA, -1)], axis=1)


def _arrange_w_ukv(w_ukv):
    w = w_ukv.astype(BF16).reshape(KV_LORA, A_HEADS, NOPE_DIM + V_DIM)
    return jnp.concatenate(
        [w[:, :, :NOPE_DIM].reshape(KV_LORA, -1), w[:, :, NOPE_DIM:].reshape(KV_LORA, -1)], axis=1)


def _pad_rows(w, start):
    full = jnp.zeros((LANES, w.shape[1]), F32).at[start:start + w.shape[0]].set(w)
    hi = full.astype(BF16)
    return jnp.concatenate([hi, hi, (full - hi.astype(F32)).astype(BF16)], axis=0)


def _row(v):
    return v.reshape(1, -1)


def kernel(x, positions, norm_ffn1, ffn1_gate, ffn1_up, ffn1_down, norm_mix, w_in, q_norm, kv_norm, w_uq, w_ukv, attn_out_norm, shift_mu, decay_w0, decay_up, iclr_a0, iclr_up, gate_up, k_k, k_a, r_k, lnx_gain, lnx_bias, conv_w, conv_out_norm, w_out, norm_ffn2, ffn2_gate, ffn2_up, ffn2_down, norm_final):
    b, s, d = x.shape
    t = b * s
    seq = lambda a: a.reshape(b, s, a.shape[-1])

    inv_freq = 1.0 / (ROPE_THETA ** (jnp.arange(0, ROPE_DIM, 2, dtype=F32) / ROPE_DIM))
    inv_lane = jnp.tile(inv_freq, LANES // inv_freq.shape[0]).reshape(1, LANES)
    tab = seq(_rope_table(positions.reshape(t, 1), inv_lane))

    h = x.reshape(t, d)
    g_final = _row(norm_final)
    ffn1 = tuple(w.astype(BF16) for w in (ffn1_gate, ffn1_up, ffn1_down))
    ffn2 = tuple(w.astype(BF16) for w in (ffn2_gate, ffn2_up, ffn2_down))
    w_out_b = w_out.astype(BF16)
    for l in range(DEPTH):
        h = _ffn(h, _row(norm_ffn1[l]), *ffn1, l, g_final, False)

        cq, ckv, pbm, pbl, pc = _in_proj(h, _row(norm_mix[l]), _arrange_w_in(w_in[l]))

        mu = shift_mu[l]
        mu_l = _spread_lora(mu[PBM_W:])
        mla_params = (_row(q_norm[l]), _row(kv_norm[l]), _arrange_w_uq(w_uq[l]), _arrange_w_ukv(w_ukv[l]))
        rw_params = (_row(mu[:PBM_W]), _row(mu_l), _row(decay_w0[l]), _row(iclr_a0[l]),
                     _pad_rows(decay_up[l], 0), _pad_rows(iclr_up[l], DECAY_LORA), _pad_rows(gate_up[l], 0),
                     _row(k_k[l]), _row(k_a[l]), _row(r_k[l]))
        (q, k, v), rw, yc = _mixer_prep(seq(cq), seq(ckv), tab, mla_params, seq(pbm), seq(pbl), rw_params,
                                        seq(pc), conv_w[l], _row(conv_out_norm[l]))
        ya = _attention(q, k, v, _row(attn_out_norm[l]))
        yb = _rwkv_scan(rw, _row(lnx_gain[l]), _row(lnx_bias[l]))

        h = _out_proj(h, ya.reshape(t, A_WIDTH), yb.reshape(t, B_WIDTH), yc.reshape(t, C_WIDTH), w_out_b, l)

        h = _ffn(h, _row(norm_ffn2[l]), *ffn2, l, g_final, l == DEPTH - 1)
    return h.reshape(b, s, d)
```

```python
import functools

import jax
import jax.numpy as jnp
from jax import lax
from jax.experimental import pallas as pl
from jax.experimental.pallas import tpu as pltpu

F32 = jnp.float32
BF16 = jnp.bfloat16

D_MODEL = 2048
DEPTH = 4
A_WIDTH = 1024
B_WIDTH = 512
C_WIDTH = 512
V_DIM = 128
NOPE_DIM = 128
ROPE_DIM = 64
QK_DIM = NOPE_DIM + ROPE_DIM
QK_PAD = 256
A_HEADS = 8
Q_LORA = 512
KV_LORA = 256
ROPE_THETA = 10000.0
B_HEAD = 64
B_HEADS = 8
DECAY_LORA = 32
ICLR_LORA = 32
GATE_LORA = 96
HEAD_NORM_EPS = 64e-5
C_GROUPS = 8
CONV_K = 3
D_FF = 5632
RMS_EPS = 1e-6
LOG2_E = 1.4426950408889634
DECAY_LOG_BOUND = 0.6065306597126334
A_IN = Q_LORA + KV_LORA + ROPE_DIM
B_IN = 3 * B_WIDTH + DECAY_LORA + ICLR_LORA + GATE_LORA

CKV_W = KV_LORA + 2 * ROPE_DIM
PBM_W = 3 * B_WIDTH
PBL_W = 256
PC_W = 3 * C_WIDTH
P_SPLITS = (Q_LORA, CKV_W, PBM_W, PBL_W, PC_W)
P_TOTAL = sum(P_SPLITS)

V7X_VMEM_BYTES = 64 * 1024 * 1024
SUBLANES = 8
LANES = 128
FFN_TM = 1024
FFN_TF = 512
PROJ_TM = 512
SCAN_TILE = 512
PREP_TILE = 512
CUM_TILE = 256
ATTN_TQ = 256
CHUNK = 64
CHUNK_SHIFT = CHUNK.bit_length() - 1
INV_BASE_BITS = 3
ROPE_TW = 3 * LANES


def _cparams(sem, vmem_mb):
    return pltpu.CompilerParams(dimension_semantics=sem, vmem_limit_bytes=vmem_mb * 1024 * 1024)


def _rms(x, g, eps=RMS_EPS):
    return x * lax.rsqrt(jnp.mean(x * x, axis=-1, keepdims=True) + eps) * g


def _dot(a, b):
    return jnp.dot(a, b, preferred_element_type=F32)


def _dot_nt(a, b):
    return lax.dot_general(a, b, (((1,), (1,)), ((), ())), preferred_element_type=F32)


def _dot_tn(a, b):
    return lax.dot_general(a, b, (((0,), (0,)), ((), ())), preferred_element_type=F32)


def _split(x):
    hi = x.astype(BF16)
    lo = (x - hi.astype(F32)).astype(BF16)
    return hi, lo


def _dot_exact_rhs(x, m2):
    hi, lo = _split(x)
    return _dot(jnp.concatenate([hi, lo], axis=1), m2)


def _dot3(x, w3):
    hi, lo = _split(x)
    return _dot(jnp.concatenate([hi, lo, hi], axis=1), w3)


def _shift_rows(prev8, x, k):
    cat = jnp.concatenate([prev8, x], axis=0)
    return pltpu.roll(cat, k, axis=0)[SUBLANES:]


def _group_ones2(n, group):
    shift = group.bit_length() - 1
    r = (lax.broadcasted_iota(jnp.int32, (2 * n, n), 0) & (n - 1)) >> shift
    c = lax.broadcasted_iota(jnp.int32, (2 * n, n), 1) >> shift
    return (r == c).astype(BF16)


def _rope_body(pos_ref, inv_ref, o_ref):
    ang = pos_ref[...].astype(F32) * inv_ref[...]
    c = jnp.cos(ang)
    s = jnp.sin(ang)
    lane = lax.broadcasted_iota(jnp.int32, ang.shape, 1)
    ssg = jnp.where((lane & (ROPE_DIM - 1)) < ROPE_DIM // 2, -s, s)
    o_ref[:, 0:LANES] = c
    o_ref[:, LANES:2 * LANES] = ssg
    o_ref[:, 2 * LANES:3 * LANES] = jnp.where(lane < ROPE_DIM, c, ssg)


def _rope_table(pos_col, inv_lane):
    t = pos_col.shape[0]
    tm = 1024
    return pl.pallas_call(
        _rope_body,
        out_shape=jax.ShapeDtypeStruct((t, ROPE_TW), F32),
        grid=(t // tm,),
        in_specs=[pl.BlockSpec((tm, 1), lambda i: (i, 0)),
                  pl.BlockSpec((1, LANES), lambda i: (0, 0))],
        out_specs=pl.BlockSpec((tm, ROPE_TW), lambda i: (i, 0)),
        compiler_params=_cparams(("parallel",), 16),
        name="rope_table",
    )(pos_col, inv_lane)


def _ffn_body(x_ref, g_ref, wg_ref, wu_ref, wd_ref, gf_ref, o_ref, xn_ref, *, final):
    j = pl.program_id(1)

    @pl.when(j == 0)
    def _():
        x = x_ref[...]
        xn_ref[...] = _rms(x, g_ref[...]).astype(BF16)
        o_ref[...] = x

    xn = xn_ref[...]
    gate = _dot(xn, wg_ref[...])
    up = _dot(xn, wu_ref[...])
    act = (gate * jax.nn.sigmoid(gate)) * (up * 0.5)
    o_ref[...] += _dot(act.astype(BF16), wd_ref[...])

    if final:
        @pl.when(j == pl.num_programs(1) - 1)
        def _():
            o_ref[...] = _rms(o_ref[...], gf_ref[...])


def _ffn(h, g, wg, wu, wd, layer, g_final, final):
    t, d = h.shape
    f = wg.shape[2]
    tm, tf = FFN_TM, FFN_TF
    return pl.pallas_call(
        functools.partial(_ffn_body, final=final),
        out_shape=jax.ShapeDtypeStruct((t, d), F32),
        grid=(t // tm, f // tf),
        in_specs=[pl.BlockSpec((tm, d), lambda i, j: (i, 0)),
                  pl.BlockSpec((1, d), lambda i, j: (0, 0)),
                  pl.BlockSpec((None, d, tf), lambda i, j: (layer, 0, j)),
                  pl.BlockSpec((None, d, tf), lambda i, j: (layer, 0, j)),
                  pl.BlockSpec((None, tf, d), lambda i, j: (layer, j, 0)),
                  pl.BlockSpec((1, d), lambda i, j: (0, 0))],
        out_specs=pl.BlockSpec((tm, d), lambda i, j: (i, 0)),
        scratch_shapes=[pltpu.VMEM((tm, d), BF16)],
        compiler_params=_cparams(("parallel", "arbitrary"), 60),
        name="ffn",
    )(h, g, wg, wu, wd, g_final)


def _inproj_body(x_ref, g_ref, w_ref, cq_ref, ckv_ref, pbm_ref, pbl_ref, pc_ref):
    xn = _rms(x_ref[...], g_ref[...]).astype(BF16)
    off = 0
    for ref in (cq_ref, ckv_ref, pbm_ref, pbl_ref, pc_ref):
        n = ref.shape[-1]
        ref[...] = _dot(xn, w_ref[:, off:off + n])
        off += n


def _in_proj(h, g, w):
    t, d = h.shape
    tm = PROJ_TM
    return pl.pallas_call(
        _inproj_body,
        out_shape=[jax.ShapeDtypeStruct((t, n), F32) for n in P_SPLITS],
        grid=(t // tm,),
        in_specs=[pl.BlockSpec((tm, d), lambda i: (i, 0)),
                  pl.BlockSpec((1, d), lambda i: (0, 0)),
                  pl.BlockSpec((d, P_TOTAL), lambda i: (0, 0), pipeline_mode=pl.Buffered(1))],
        out_specs=[pl.BlockSpec((tm, n), lambda i: (i, 0)) for n in P_SPLITS],
        compiler_params=_cparams(("parallel",), 56),
        name="in_proj",
    )(h, g, w)


def _mla_prep_body(cq_ref, ckv_ref, tab_ref, qn_ref, kvn_ref, wq_ref, wkv_ref, q_ref, k_ref, v_ref):
    nq = A_HEADS * NOPE_DIM
    nr = A_HEADS * ROPE_DIM
    scale = QK_DIM ** -0.5 * LOG2_E
    tab = tab_ref[0]
    cos2 = tab[:, 0:LANES]
    sin2 = tab[:, LANES:2 * LANES]
    cs = tab[:, 2 * LANES:3 * LANES]

    cqn = _rms(cq_ref[0], qn_ref[...]).astype(BF16)
    qa = _dot(cqn, wq_ref[...])
    cos_h = jnp.concatenate([cos2] * (nr // LANES), axis=1)
    sin_h = jnp.concatenate([sin2] * (nr // LANES), axis=1)
    q_rot = qa[:, nq:nq + nr] * cos_h + qa[:, nq + nr:nq + 2 * nr] * sin_h

    ckv = ckv_ref[0]
    ckvn = _rms(ckv[:, :KV_LORA], kvn_ref[...]).astype(BF16)
    kv = _dot(ckvn, wkv_ref[...])
    t = ckv[:, KV_LORA:KV_LORA + LANES] * cs
    low_half = lax.broadcasted_iota(jnp.int32, t.shape, 1) < ROPE_DIM
    k_rot = jnp.where(low_half, t + pltpu.roll(t, ROPE_DIM, axis=1), 0.0).astype(BF16)
    for h in range(A_HEADS):
        pair = q_rot[:, (h // 2) * LANES:(h // 2 + 1) * LANES]
        if h % 2:
            pair = pltpu.roll(pair, ROPE_DIM, axis=1)
        q_rope = jnp.where(low_half, pair, 0.0)
        q_nope = qa[:, h * NOPE_DIM:(h + 1) * NOPE_DIM]
        q_ref[0, h] = (jnp.concatenate([q_nope, q_rope], axis=1) * scale).astype(BF16)
        k_ref[0, h] = jnp.concatenate([kv[:, h * NOPE_DIM:(h + 1) * NOPE_DIM].astype(BF16), k_rot], axis=1)
        v_ref[0, h] = kv[:, nq + h * V_DIM:nq + (h + 1) * V_DIM].astype(BF16)


def _mla_prep_specs(b, s, ts, qn, kvn, wq, wkv):
    tok = lambda n: pl.BlockSpec((1, ts, n), lambda bi, i: (bi, i, 0))
    full = lambda a: pl.BlockSpec(a.shape, lambda bi, i: (0, 0))
    head = lambda n: pl.BlockSpec((1, A_HEADS, ts, n), lambda bi, i: (bi, 0, i, 0))
    in_specs = [tok(Q_LORA), tok(CKV_W), tok(ROPE_TW), full(qn), full(kvn), full(wq), full(wkv)]
    out_shape = [jax.ShapeDtypeStruct((b, A_HEADS, s, QK_PAD), BF16),
                 jax.ShapeDtypeStruct((b, A_HEADS, s, QK_PAD), BF16),
                 jax.ShapeDtypeStruct((b, A_HEADS, s, V_DIM), BF16)]
    return in_specs, out_shape, [head(QK_PAD), head(QK_PAD), head(V_DIM)]


def _attn_body(q_ref, k_ref, v_ref, g_ref, o_ref, s_ref, mx_ref, m_ref, acc_ref):
    i = pl.program_id(1)
    nh = q_ref.shape[1]
    tq = q_ref.shape[2]
    heads = range(nh)
    ones = jnp.ones((tq, V_DIM), BF16)
    keys = lambda j: pl.ds(pl.multiple_of(j * tq, tq), tq)

    row = lax.broadcasted_iota(jnp.int32, (tq, tq), 0)
    col = lax.broadcasted_iota(jnp.int32, (tq, tq), 1)
    for h in heads:
        s = jnp.where(col <= row, _dot_nt(q_ref[0, h], k_ref[0, h, keys(i), :]), -1e30)
        s_ref[h, i] = s
        mx_ref[h] = s

    def scores(blocks):
        for h in heads:
            s = [_dot_nt(q_ref[0, h], k_ref[0, h, keys(j), :]) for j in blocks]
            for j, s_j in zip(blocks, s):
                s_ref[h, j] = s_j
            mx_ref[h] = jnp.maximum(mx_ref[h], functools.reduce(jnp.maximum, s))

    def score_quad(jj, _):
        scores(tuple(4 * jj + t for t in range(4)))
        return 0

    lax.fori_loop(0, i >> 2, score_quad, 0)
    rem = (i >> 2) << 2

    @pl.when((i & 2) == 2)
    def _():
        scores((rem, rem + 1))

    @pl.when((i & 1) == 1)
    def _():
        scores((i - 1,))

    for h in heads:
        m_ref[h] = jnp.broadcast_to(jnp.max(mx_ref[h], axis=-1, keepdims=True), (tq, LANES))

    def weighted(j, h):
        m = m_ref[h]
        p = jnp.exp2(s_ref[h, j] - jnp.concatenate([m] * (tq // LANES), axis=1))
        return _dot(p.astype(BF16), jnp.concatenate([v_ref[0, h, keys(j), :], ones], axis=1))

    for h in heads:
        acc_ref[h] = weighted(i, h)

    def values(blocks):
        for h in heads:
            acc_ref[h] += functools.reduce(lambda a, b: a + b, [weighted(j, h) for j in blocks])

    def value_quad(jj, _):
        values(tuple(4 * jj + t for t in range(4)))
        return 0

    lax.fori_loop(0, i >> 2, value_quad, 0)

    @pl.when((i & 2) == 2)
    def _():
        values((rem, rem + 1))

    @pl.when((i & 1) == 1)
    def _():
        values((i - 1,))

    for h in heads:
        acc = acc_ref[h]
        hs = slice(h * V_DIM, (h + 1) * V_DIM)
        o_ref[0, :, hs] = _rms(acc[:, :V_DIM] / acc[:, V_DIM:], g_ref[:, hs]).astype(o_ref.dtype)


def _attention(q, k, v, g):
    b, nh, s, _ = q.shape
    tq = ATTN_TQ
    return pl.pallas_call(
        _attn_body,
        out_shape=jax.ShapeDtypeStruct((b, s, nh * V_DIM), BF16),
        grid=(b, s // tq),
        in_specs=[pl.BlockSpec((1, nh, tq, QK_PAD), lambda bi, i: (bi, 0, i, 0)),
                  pl.BlockSpec((1, nh, s, QK_PAD), lambda bi, i: (bi, 0, 0, 0)),
                  pl.BlockSpec((1, nh, s, V_DIM), lambda bi, i: (bi, 0, 0, 0)),
                  pl.BlockSpec((1, nh * V_DIM), lambda bi, i: (0, 0))],
        out_specs=pl.BlockSpec((1, tq, nh * V_DIM), lambda bi, i: (bi, i, 0)),
        scratch_shapes=[pltpu.VMEM((nh, s // tq, tq, tq), F32),
                        pltpu.VMEM((nh, tq, tq), F32),
                        pltpu.VMEM((nh, tq, LANES), F32),
                        pltpu.VMEM((nh, tq, 2 * V_DIM), F32)],
        compiler_params=_cparams(("parallel", "arbitrary"), 56),
        name="mla_attention",
    )(q, k, v, g)


def _rwkv_prep_body(pbm_ref, pbl_ref, pbm_prev_ref, pbl_prev_ref, mum_ref, mul_ref, w0_ref, a0_ref,
                    dec_ref, icl_ref, gat_ref, kk_ref, ka_ref, rk_ref,
                    at_ref, rt_ref, bt_ref, kt_ref, bh_ref, kh_ref, v_ref, g_ref, bonus_ref, ptot_ref):
    i = pl.program_id(1)
    ts = pbm_ref.shape[1]
    w = B_WIDTH
    first = i == 0

    xm = pbm_ref[0]
    xl = pbl_ref[0]
    pm = jnp.where(first, 0.0, pbm_prev_ref[0])
    pv = jnp.where(first, 0.0, pbl_prev_ref[0])
    xs = xm + (_shift_rows(pm, xm, 1) - xm) * mum_ref[...]
    ls = xl + (_shift_rows(pv, xl, 1) - xl) * mul_ref[...]
    r = xs[:, 0:w]
    k = xs[:, w:2 * w]
    v = xs[:, 2 * w:3 * w]

    ls_wa = ls[:, :LANES]
    wl = w0_ref[...] + _dot3(jnp.tanh(ls_wa), dec_ref[...])
    lw = (-DECAY_LOG_BOUND * LOG2_E) * jax.nn.sigmoid(wl)
    a_ic = jax.nn.sigmoid(a0_ref[...] + _dot3(ls_wa, icl_ref[...]))
    gate = _dot3(jax.nn.sigmoid(ls[:, LANES:]), gat_ref[...])

    seg2 = _group_ones2(w, B_HEAD)
    kk = k * kk_ref[...]
    kkn = kk * jnp.minimum(lax.rsqrt(_dot_exact_rhs(kk * kk, seg2)), 1e12)
    kp = k * (1.0 + (a_ic - 1.0) * ka_ref[...])
    bonus = _dot_exact_rhs(r * kp * rk_ref[...], seg2) * v

    ct = CUM_TILE
    row = lax.broadcasted_iota(jnp.int32, (2 * ct, 2 * ct), 0)
    col = lax.broadcasted_iota(jnp.int32, (2 * ct, 2 * ct), 1) & (ct - 1)
    same = ((row & (ct - 1)) >> CHUNK_SHIFT) == (col >> CHUNK_SHIFT)
    pos = row & (ct - 1)
    first_col = jnp.where(row < ct, 0, pos + 1)
    last_col = jnp.where(row < ct, pos, ct)
    tri2 = (same & (col >= first_col) & (col <= last_col)).astype(BF16)
    lw_hi, lw_lo = _split(lw)
    cums = [_dot(tri2, jnp.concatenate([lw_hi[t0:t0 + ct], lw_lo[t0:t0 + ct]], axis=0))
            for t0 in range(0, ts, ct)]
    cum = jnp.concatenate([c[:ct] for c in cums], axis=0)
    cumr = jnp.concatenate([c[ct:] for c in cums], axis=0)

    e_neg = jnp.exp2(-cum)
    e_rem = jnp.exp2(cumr)
    b_vec = kkn * a_ic
    at_ref[0] = (-kkn * jnp.exp2(cum - lw)).astype(BF16)
    rt_ref[0] = (r * jnp.exp2(cum)).astype(BF16)
    bt_ref[0] = (b_vec * e_neg).astype(BF16)
    kt_ref[0] = (kp * e_neg).astype(BF16)
    bh_ref[0] = (b_vec * e_rem).astype(BF16)
    kh_ref[0] = (kp * e_rem).astype(BF16)
    v_ref[0] = v.astype(BF16)
    g_ref[0] = gate
    bonus_ref[0] = bonus
    nc = ptot_ref.shape[2]
    for t in range(ptot_ref.shape[1]):
        last_rows = [(t * nc + c + 1) * CHUNK - 1 for c in range(nc)]
        ptot_ref[0, t] = jnp.exp2(jnp.concatenate([cum[r0:r0 + 1] for r0 in last_rows], axis=0))


def _rwkv_prep_specs(b, s, ts, params):
    nc = SCAN_TILE // CHUNK
    nt = ts // SCAN_TILE
    w = B_WIDTH
    tok = lambda n: pl.BlockSpec((1, ts, n), lambda bi, i: (bi, i, 0))
    prev = lambda n: pl.BlockSpec(
        (1, SUBLANES, n), lambda bi, i: (bi, jnp.maximum(i * (ts // SUBLANES) - 1, 0), 0))
    full = lambda a: pl.BlockSpec(a.shape, lambda bi, i: (0, 0))
    in_specs = [tok(PBM_W), tok(PBL_W), prev(PBM_W), prev(PBL_W)] + [full(p) for p in params]
    out_shape = ([jax.ShapeDtypeStruct((b, s, w), BF16)] * 7 + [jax.ShapeDtypeStruct((b, s, w), F32)] * 2
                 + [jax.ShapeDtypeStruct((b, s // SCAN_TILE, nc, w), F32)])
    out_specs = [tok(w)] * 9 + [pl.BlockSpec((1, nt, nc, w), lambda bi, i: (bi, i, 0, 0))]
    return in_specs, out_shape, out_specs


def _rwkv_scan_body(at_ref, rt_ref, bt_ref, kt_ref, bh_ref, kh_ref, v_ref, g_ref, bonus_ref, ptot_ref,
                    gain_ref, bias_ref, o_ref, state_ref):
    n = B_HEAD
    c_len = CHUNK
    pw = 2 * n
    ts = at_ref.shape[1]

    @pl.when(pl.program_id(1) == 0)
    def _():
        state_ref[...] = jnp.zeros_like(state_ref)

    row = lax.broadcasted_iota(jnp.int32, (c_len, pw), 0)
    lane = lax.broadcasted_iota(jnp.int32, (c_len, pw), 1)
    col = lane & (n - 1)
    strict = col < row
    incl = col <= row
    eye2 = (col == row).astype(F32)
    same_block = lambda bits: (row >> bits) == (col >> bits)
    diag_blocks = same_block(INV_BASE_BITS)
    merge_masks = [same_block(bits + 1) & ~same_block(bits) for bits in range(INV_BASE_BITS, CHUNK_SHIFT)]
    first = lane < n
    half_f = (first.astype(F32), 1.0 - first.astype(F32))
    half_b = tuple(hm[:1].astype(BF16) for hm in half_f)
    sq_r = lax.broadcasted_iota(jnp.int32, (pw, pw), 0)
    sq_c = lax.broadcasted_iota(jnp.int32, (pw, pw), 1)
    pair_blocks = (sq_r >> CHUNK_SHIFT) == (sq_c >> CHUNK_SHIFT)
    pair_eye = sq_r == sq_c

    def bd(y):
        return jnp.concatenate([y * half_b[0], y * half_b[1]], axis=0)

    def mm(x, y_bd):
        return _dot(x.astype(BF16), y_bd)

    n_chunks = ts // c_len
    pairs = range(at_ref.shape[2] // pw)
    units = [(c, q) for c in range(n_chunks) for q in pairs]
    rows = lambda c: slice(c * c_len, (c + 1) * c_len)
    lanes = lambda q: slice(q * pw, (q + 1) * pw)
    load = lambda ref: {(c, q): ref[0, rows(c), lanes(q)] for c, q in units}
    at, rt, bt, kt, bh, kh, vv = (load(r) for r in (at_ref, rt_ref, bt_ref, kt_ref, bh_ref, kh_ref, v_ref))
    state = [state_ref[q] for q in pairs]

    ar = {u: jnp.concatenate([at[u], rt[u]], axis=0) for u in units}
    g_b = {u: _dot_nt(ar[u], bd(bt[u])) for u in units}
    g_k = {u: _dot_nt(ar[u], bd(kt[u])) for u in units}
    l_mat = {u: jnp.where(strict, g_b[u][:c_len], 0.0) for u in units}
    a_ak = {u: jnp.where(strict, g_k[u][:c_len], 0.0).astype(BF16) for u in units}
    a_rb = {u: jnp.where(incl, g_b[u][c_len:], 0.0).astype(BF16) for u in units}
    a_rk = {u: jnp.where(incl, g_k[u][c_len:], 0.0).astype(BF16) for u in units}

    p = {u: jnp.where(diag_blocks, l_mat[u], 0.0) for u in units}
    t_inv = {u: eye2 + p[u] for u in units}
    p = {u: p[u].astype(BF16) for u in units}
    for _ in range(INV_BASE_BITS - 1):
        p = {u: mm(p[u], bd(p[u])).astype(BF16) for u in units}
        t_inv = {u: t_inv[u] + mm(t_inv[u], bd(p[u])) for u in units}
    for off_mask in merge_masks:
        t_bd = {u: bd(t_inv[u].astype(BF16)) for u in units}
        t_l = {u: mm(t_inv[u], bd(jnp.where(off_mask, l_mat[u], 0.0).astype(BF16))) for u in units}
        t_inv = {u: t_inv[u] + mm(t_l[u], t_bd[u]) for u in units}

    v_bd = {u: bd(vv[u]) for u in units}
    akv = {u: mm(a_ak[u], v_bd[u]).astype(BF16) for u in units}
    ap = {u: mm(t_inv[u], bd(at[u])).astype(BF16) for u in units}
    ut = {u: mm(t_inv[u], bd(akv[u])).astype(BF16) for u in units}
    ap_bd = {u: bd(ap[u]) for u in units}
    rp = {u: (rt[u].astype(F32) + mm(a_rb[u], ap_bd[u])).astype(BF16) for u in units}
    y_loc = {u: mm(a_rb[u], bd(ut[u])) + mm(a_rk[u], v_bd[u]) for u in units}
    w_m = {u: _dot_tn(ap[u], bh[u]) for u in units}
    w_g = {u: _dot_tn(jnp.concatenate([ut[u], vv[u]], axis=0), jnp.concatenate([bh[u], kh[u]], axis=0))
           for u in units}
    m_bd = {(c, q): (jnp.where(pair_blocks, w_m[(c, q)], 0.0)
                     + jnp.where(pair_eye, ptot_ref[0, 0, c:c + 1, lanes(q)], 0.0)).astype(BF16)
            for c, q in units}
    g_c = {u: w_g[u][:n] * half_f[0] + w_g[u][n:] * half_f[1] for u in units}

    y = {}
    for c, q in units:
        s0 = state[q].astype(BF16)
        y[(c, q)] = _dot_nt(rp[(c, q)], bd(s0)) + y_loc[(c, q)]
        state[q] = _dot(s0, m_bd[(c, q)]) + g_c[(c, q)]

    for c, q in units:
        yv = y[(c, q)]
        head_mean = lambda z: jnp.where(first, jnp.sum(z * half_f[0], axis=-1, keepdims=True),
                                        jnp.sum(z * half_f[1], axis=-1, keepdims=True)) * (1.0 / n)
        yc = yv - head_mean(yv)
        var = head_mean(yc * yc)
        yn = yc * lax.rsqrt(var + HEAD_NORM_EPS) * gain_ref[:, lanes(q)] + bias_ref[:, lanes(q)]
        o_ref[0, rows(c), lanes(q)] = ((yn + bonus_ref[0, rows(c), lanes(q)])
                                       * g_ref[0, rows(c), lanes(q)]).astype(o_ref.dtype)
    for q in pairs:
        state_ref[q] = state[q]


def _rwkv_scan(arrs, gain, bias):
    at = arrs[0]
    b, s, w = at.shape
    ts = SCAN_TILE
    nc = ts // CHUNK
    tok = pl.BlockSpec((1, ts, w), lambda bi, i: (bi, i, 0))
    vec = pl.BlockSpec((1, w), lambda bi, i: (0, 0))
    return pl.pallas_call(
        _rwkv_scan_body,
        out_shape=jax.ShapeDtypeStruct((b, s, w), BF16),
        grid=(b, s // ts),
        in_specs=[tok] * 9 + [pl.BlockSpec((1, 1, nc, w), lambda bi, i: (bi, i, 0, 0)), vec, vec],
        out_specs=tok,
        scratch_shapes=[pltpu.VMEM((w // (2 * B_HEAD), B_HEAD, 2 * B_HEAD), F32)],
        compiler_params=_cparams(("parallel", "arbitrary"), 32),
        name="rwkv_scan",
    )(*arrs, gain, bias)


def _conv_body(pc_ref, prev_ref, w_ref, g_ref, o_ref):
    w = C_WIDTH
    x = pc_ref[0]
    xp = jnp.where(pl.program_id(1) == 0, 0.0, prev_ref[0])
    u = x[:, w:2 * w] * x[:, 2 * w:3 * w]
    up = xp[:, w:2 * w] * xp[:, 2 * w:3 * w]
    cw = w_ref[...]
    y = cw[0:1] * _shift_rows(up, u, 2) + cw[1:2] * _shift_rows(up, u, 1) + cw[2:3] * u
    z = x[:, 0:w] * y
    ms = _dot_exact_rhs(z * z, _group_ones2(w, w // C_GROUPS)) * (C_GROUPS / w)
    o_ref[0] = (z * lax.rsqrt(ms + RMS_EPS) * g_ref[...]).astype(o_ref.dtype)


def _conv_specs(b, s, ts, conv_w, gain):
    in_specs = [pl.BlockSpec((1, ts, PC_W), lambda bi, i: (bi, i, 0)),
                pl.BlockSpec((1, SUBLANES, PC_W),
                             lambda bi, i: (bi, jnp.maximum(i * (ts // SUBLANES) - 1, 0), 0)),
                pl.BlockSpec(conv_w.shape, lambda bi, i: (0, 0)),
                pl.BlockSpec(gain.shape, lambda bi, i: (0, 0))]
    out_shape = [jax.ShapeDtypeStruct((b, s, C_WIDTH), BF16)]
    return in_specs, out_shape, [pl.BlockSpec((1, ts, C_WIDTH), lambda bi, i: (bi, i, 0))]


def _mixer_prep_body(*refs, n_in, n_out):
    ins, outs = refs[:sum(n_in)], refs[sum(n_in):]
    bodies = (_mla_prep_body, _rwkv_prep_body, _conv_body)
    i0 = o0 = 0
    for body, ni, no in zip(bodies, n_in, n_out):
        body(*ins[i0:i0 + ni], *outs[o0:o0 + no])
        i0 += ni
        o0 += no


def _mixer_prep(cq, ckv, tab, mla_params, pbm, pbl, rw_params, pc, conv_w, conv_gain):
    b, s, _ = cq.shape
    ts = PREP_TILE
    parts = (_mla_prep_specs(b, s, ts, *mla_params), _rwkv_prep_specs(b, s, ts, rw_params),
             _conv_specs(b, s, ts, conv_w, conv_gain))
    outs = pl.pallas_call(
        functools.partial(_mixer_prep_body, n_in=tuple(len(p[0]) for p in parts),
                          n_out=tuple(len(p[1]) for p in parts)),
        out_shape=[sh for p in parts for sh in p[1]],
        grid=(b, s // ts),
        in_specs=[sp for p in parts for sp in p[0]],
        out_specs=[sp for p in parts for sp in p[2]],
        compiler_params=_cparams(("parallel", "parallel"), 60),
        name="mixer_prep",
    )(cq, ckv, tab, *mla_params, pbm, pbl, pbm, pbl, *rw_params, pc, pc, conv_w, conv_gain)
    return outs[:3], outs[3:13], outs[13]


def _outproj_body(h_ref, ya_ref, yb_ref, yc_ref, w_ref, o_ref):
    acc = h_ref[...] + _dot(ya_ref[...], w_ref[0:A_WIDTH, :])
    acc = acc + _dot(yb_ref[...], w_ref[A_WIDTH:A_WIDTH + B_WIDTH, :])
    o_ref[...] = acc + _dot(yc_ref[...], w_ref[A_WIDTH + B_WIDTH:, :])


def _out_proj(h, ya, yb, yc, w, layer):
    t, d = h.shape
    tm = PROJ_TM
    row = lambda n: pl.BlockSpec((tm, n), lambda i: (i, 0))
    return pl.pallas_call(
        _outproj_body,
        out_shape=jax.ShapeDtypeStruct((t, d), F32),
        grid=(t // tm,),
        in_specs=[row(d), row(A_WIDTH), row(B_WIDTH), row(C_WIDTH),
                  pl.BlockSpec((None,) + w.shape[1:], lambda i: (layer, 0, 0))],
        out_specs=row(d),
        compiler_params=_cparams(("parallel",), 48),
        name="out_proj",
    )(h, ya, yb, yc, w)


def _swap_halves(w):
    half = ROPE_DIM // 2
    return jnp.concatenate([w[..., half:], w[..., :half]], axis=-1)


def _arrange_w_in(w_in):
    w_in = w_in.astype(BF16)
    o = Q_LORA + KV_LORA
    k_rope = w_in[:, o:A_IN]
    b0 = A_IN
    return jnp.concatenate(
        [w_in[:, :Q_LORA], w_in[:, Q_LORA:o], k_rope, _swap_halves(k_rope),
         w_in[:, b0:b0 + PBM_W], _spread_lora(w_in[:, b0 + PBM_W:b0 + B_IN]), w_in[:, b0 + B_IN:]], axis=1)


def _spread_lora(cols):
    n_wa = DECAY_LORA + ICLR_LORA
    zeros = lambda n: jnp.zeros(cols.shape[:-1] + (n,), cols.dtype)
    return jnp.concatenate([cols[..., :n_wa], zeros(LANES - n_wa), cols[..., n_wa:], zeros(LANES - GATE_LORA)],
                           axis=-1)


def _arrange_w_uq(w_uq):
    w = w_uq.astype(BF16).reshape(Q_LORA, A_HEADS, QK_DIM)
    nope = w[:, :, :NOPE_DIM].reshape(Q_LORA, -1)
    rope = w[:, :, NOPE_DIM:]
    return jnp.concatenate(
        [nope, rope.reshape(Q_LORA, -1), _swap_halves(rope).reshape(Q_LORA, -1)], axis=1)


def _arrange_w_ukv(w_ukv):
    w = w_ukv.astype(BF16).reshape(KV_LORA, A_HEADS, NOPE_DIM + V_DIM)
    return jnp.concatenate(
        [w[:, :, :NOPE_DIM].reshape(KV_LORA, -1), w[:, :, NOPE_DIM:].reshape(KV_LORA, -1)], axis=1)


def _pad_rows(w, start):
    full = jnp.zeros((LANES, w.shape[1]), F32).at[start:start + w.shape[0]].set(w)
    hi = full.astype(BF16)
    return jnp.concatenate([hi, hi, (full - hi.astype(F32)).astype(BF16)], axis=0)


def _row(v):
    return v.reshape(1, -1)


def kernel(x, positions, norm_ffn1, ffn1_gate, ffn1_up, ffn1_down, norm_mix, w_in, q_norm, kv_norm, w_uq, w_ukv, attn_out_norm, shift_mu, decay_w0, decay_up, iclr_a0, iclr_up, gate_up, k_k, k_a, r_k, lnx_gain, lnx_bias, conv_w, conv_out_norm, w_out, norm_ffn2, ffn2_gate, ffn2_up, ffn2_down, norm_final):
    b, s, d = x.shape
    t = b * s
    seq = lambda a: a.reshape(b, s, a.shape[-1])

    inv_freq = 1.0 / (ROPE_THETA ** (jnp.arange(0, ROPE_DIM, 2, dtype=F32) / ROPE_DIM))
    inv_lane = jnp.tile(inv_freq, LANES // inv_freq.shape[0]).reshape(1, LANES)
    tab = seq(_rope_table(positions.reshape(t, 1), inv_lane))

    h = x.reshape(t, d)
    g_final = _row(norm_final)
    ffn1 = tuple(w.astype(BF16) for w in (ffn1_gate, ffn1_up, ffn1_down))
    ffn2 = tuple(w.astype(BF16) for w in (ffn2_gate, ffn2_up, ffn2_down))
    w_out_b = w_out.astype(BF16)
    for l in range(DEPTH):
        h = _ffn(h, _row(norm_ffn1[l]), *ffn1, l, g_final, False)

        cq, ckv, pbm, pbl, pc = _in_proj(h, _row(norm_mix[l]), _arrange_w_in(w_in[l]))

        mu = shift_mu[l]
        mu_l = _spread_lora(mu[PBM_W:])
        mla_params = (_row(q_norm[l]), _row(kv_norm[l]), _arrange_w_uq(w_uq[l]), _arrange_w_ukv(w_ukv[l]))
        rw_params = (_row(mu[:PBM_W]), _row(mu_l), _row(decay_w0[l]), _row(iclr_a0[l]),
                     _pad_rows(decay_up[l], 0), _pad_rows(iclr_up[l], DECAY_LORA), _pad_rows(gate_up[l], 0),
                     _row(k_k[l]), _row(k_a[l]), _row(r_k[l]))
        (q, k, v), rw, yc = _mixer_prep(seq(cq), seq(ckv), tab, mla_params, seq(pbm), seq(pbl), rw_params,
                                        seq(pc), conv_w[l], _row(conv_out_norm[l]))
        ya = _attention(q, k, v, _row(attn_out_norm[l]))
        yb = _rwkv_scan(rw, _row(lnx_gain[l]), _row(lnx_bias[l]))

        h = _out_proj(h, ya.reshape(t, A_WIDTH), yb.reshape(t, B_WIDTH), yc.reshape(t, C_WIDTH), w_out_b, l)

        h = _ffn(h, _row(norm_ffn2[l]), *ffn2, l, g_final, l == DEPTH - 1)
    return h.reshape(b, s, d)
```
